```python
import jax
import jax.numpy as jnp
from jax import lax
import numpy as np

D_MODEL = 1024
BATCH = 8
SEQ = 4096
DEPTH = 4

N_EVEN = (DEPTH + 1) // 2
N_ODD = DEPTH // 2
D_FF = 4 * D_MODEL
NORM_EPS = 1e-6

SC_WIDTH = D_MODEL // 2
SC_KERNEL = 3
MLA_HEADS = 8
MLA_NOPE = 64
MLA_ROPE = 32
MLA_V = 64
MLA_Q_RANK = 384
MLA_KV_RANK = 256
ROPE_THETA = 10000.0
ATTN_BLOCK = 128
EVEN_SPLITS = (SC_WIDTH, SC_WIDTH, SC_WIDTH, MLA_Q_RANK, MLA_KV_RANK, MLA_ROPE)
EVEN_IN = sum(EVEN_SPLITS)
EVEN_MIX = SC_WIDTH + MLA_HEADS * MLA_V

GLA_HEADS = 4
GLA_DK = 64
GLA_DV = 128
GLA_GATE_RANK = 16
GLA_TAU = 16.0
GLA_CHUNK = 64
LRU_WIDTH = D_MODEL // 2
LRU_BLOCKS = 8
LRU_KERNEL = 4
LRU_C = 8.0
ODD_SPLITS = (GLA_HEADS * GLA_DK, GLA_HEADS * GLA_DK, GLA_HEADS * GLA_DV, GLA_GATE_RANK,
              GLA_HEADS * GLA_DV, LRU_WIDTH, LRU_WIDTH)
ODD_IN = sum(ODD_SPLITS)
ODD_MIX = GLA_HEADS * GLA_DV + LRU_WIDTH

kernel_name = 'hybrid_sconv_mla_gla_rglru_trunk'


def _split(t, sizes):
    return jnp.split(t, np.cumsum(sizes)[:-1].tolist(), axis=-1)


def rmsnorm(x, g):
    xf = x.astype(jnp.float32)
    y = xf * lax.rsqrt(jnp.mean(xf * xf, axis=-1, keepdims=True) + NORM_EPS)
    return (y * g.astype(jnp.float32)).astype(x.dtype)


def causal_dwconv(u, w):
    k = w.shape[0]
    return lax.conv_general_dilated(
        u, w[:, None, :].astype(u.dtype), window_strides=(1,), padding=[(k - 1, 0)],
        dimension_numbers=('NWC', 'WIO', 'NWC'), feature_group_count=u.shape[-1])


def apply_rope(x, cos, sin):
    half = x.shape[-1] // 2
    x1, x2 = x[..., :half], x[..., half:]
    return jnp.concatenate([x1 * cos - x2 * sin, x2 * cos + x1 * sin], axis=-1).astype(x.dtype)


def mla_attention(c_q, c_kv, k_rope_raw, positions, q_norm_g, w_uq, kv_norm_g, w_ukv):
    b, s, _ = c_q.shape
    q = (rmsnorm(c_q, q_norm_g) @ w_uq).reshape(b, s, MLA_HEADS, MLA_NOPE + MLA_ROPE)
    q_nope, q_rope = q[..., :MLA_NOPE], q[..., MLA_NOPE:]
    kv = (rmsnorm(c_kv, kv_norm_g) @ w_ukv).reshape(b, s, MLA_HEADS, MLA_NOPE + MLA_V)
    k_nope, v = kv[..., :MLA_NOPE], kv[..., MLA_NOPE:]
    inv_freq = ROPE_THETA ** (-jnp.arange(0, MLA_ROPE, 2, dtype=jnp.float32) / MLA_ROPE)
    ang = positions[..., None].astype(jnp.float32) * inv_freq
    cos, sin = jnp.cos(ang), jnp.sin(ang)
    q_rope = apply_rope(q_rope, cos[:, :, None, :], sin[:, :, None, :])
    k_rope = apply_rope(k_rope_raw, cos, sin)
    scale = (MLA_NOPE + MLA_ROPE) ** -0.5
    outs = []
    for start in range(0, s, ATTN_BLOCK):
        end = start + ATTN_BLOCK
        sc = (jnp.einsum('bqhd,bkhd->bhqk', q_nope[:, start:end], k_nope[:, :end])
              + jnp.einsum('bqhr,bkr->bhqk', q_rope[:, start:end], k_rope[:, :end]))
        sc = sc.astype(jnp.float32) * scale
        causal = jnp.arange(end)[None, :] <= jnp.arange(start, end)[:, None]
        p = jax.nn.softmax(jnp.where(causal, sc, -jnp.inf), axis=-1).astype(v.dtype)
        outs.append(jnp.einsum('bhqk,bkhd->bqhd', p, v[:, :end]))
    o = jnp.concatenate(outs, axis=1)
    return o.reshape(b, s, MLA_HEADS * MLA_V)


def even_mixer(h, positions, w_in, conv_w, q_norm_g, w_uq, kv_norm_g, w_ukv, w_out):
    gate_b, gate_c, u, c_q, c_kv, k_rope = _split(h @ w_in, EVEN_SPLITS)
    sc_out = gate_b * causal_dwconv(gate_c * u, conv_w)
    mla_out = mla_attention(c_q, c_kv, k_rope, positions, q_norm_g, w_uq, kv_norm_g, w_ukv)
    return jnp.concatenate([sc_out, mla_out.astype(h.dtype)], axis=-1) @ w_out


def gla_chunked(q, k, v, log_a):
    b, s = q.shape[:2]
    n = s // GLA_CHUNK

    def chunk(t):
        return t.reshape(b, n, GLA_CHUNK, GLA_HEADS, t.shape[-1]).transpose(0, 3, 1, 2, 4)

    qc = chunk(q) * (GLA_DK ** -0.5)
    kc = chunk(k)
    vc = chunk(v)
    cum = jnp.cumsum(chunk(log_a).astype(jnp.float32), axis=3)
    last = cum[:, :, :, -1:, :]
    q_in = qc * jnp.exp(cum)
    k_in = kc * jnp.exp(-cum)
    k_out = kc * jnp.exp(last - cum)
    mask = jnp.tril(jnp.ones((GLA_CHUNK, GLA_CHUNK), dtype=bool))
    att = jnp.where(mask, jnp.einsum('bhnid,bhnjd->bhnij', q_in, k_in), 0.0)
    o_intra = jnp.einsum('bhnij,bhnjv->bhniv', att, vc)
    kv = jnp.einsum('bhnjd,bhnjv->bhndv', k_out, vc).astype(jnp.float32)
    decay = jnp.exp(last[:, :, :, 0, :])

    def step(state, inp):
        d, kv_n = inp
        return d[..., None] * state + kv_n, state

    init = jnp.zeros((b, GLA_HEADS, GLA_DK, GLA_DV), jnp.float32)
    _, states = lax.scan(step, init, (jnp.moveaxis(decay, 2, 0), jnp.moveaxis(kv, 2, 0)))
    states = jnp.moveaxis(states, 0, 2)
    o_inter = jnp.einsum('bhnid,bhndv->bhniv', q_in, states)
    o = o_intra + o_inter
    return o.transpose(0, 2, 3, 1, 4).reshape(b, s, GLA_HEADS, GLA_DV)


def rglru(xc, w_a, b_a, w_i, b_i, lam):
    b, s, c = xc.shape
    xb = xc.reshape(b, s, LRU_BLOCKS, c // LRU_BLOCKS)
    r = jax.nn.sigmoid(jnp.einsum('bsnc,ncd->bsnd', xb, w_a).reshape(b, s, c) + b_a)
    i = jax.nn.sigmoid(jnp.einsum('bsnc,ncd->bsnd', xb, w_i).reshape(b, s, c) + b_i)
    log_a = (-LRU_C * r.astype(jnp.float32)) * jax.nn.softplus(-lam.astype(jnp.float32))
    a = jnp.exp(log_a)
    inp = jnp.sqrt(-jnp.expm1(2.0 * log_a)) * (i * xc).astype(jnp.float32)

    def combine(left, right):
        a1, b1 = left
        a2, b2 = right
        return a1 * a2, a2 * b1 + b2

    _, hseq = lax.associative_scan(combine, (a, inp), axis=1)
    return hseq.astype(xc.dtype)


def odd_mixer(h, w_in, gate_up, gate_bias, gla_norm_g, conv_w, conv_b, w_a, b_a, w_i, b_i, lam, w_out):
    b, s, _ = h.shape
    q, k, v, g_lr, o_gate, x_lru, y_lru = _split(h @ w_in, ODD_SPLITS)
    log_a = jax.nn.log_sigmoid((g_lr @ gate_up + gate_bias).astype(jnp.float32)) / GLA_TAU
    o = gla_chunked(q.reshape(b, s, GLA_HEADS, GLA_DK), k.reshape(b, s, GLA_HEADS, GLA_DK),
                    v.reshape(b, s, GLA_HEADS, GLA_DV), log_a.reshape(b, s, GLA_HEADS, GLA_DK))
    o = rmsnorm(o, gla_norm_g).reshape(b, s, GLA_HEADS * GLA_DV)
    gla_out = (o * jax.nn.silu(o_gate)).astype(h.dtype)
    xc = causal_dwconv(x_lru, conv_w) + conv_b
    lru_out = jax.nn.gelu(y_lru) * rglru(xc, w_a, b_a, w_i, b_i, lam)
    return jnp.concatenate([gla_out, lru_out.astype(h.dtype)], axis=-1) @ w_out


def setup_inputs(seed: int = 0) -> dict:
    key = jax.random.key(seed)
    ks = list(jax.random.split(key, 40))
    f32 = jnp.float32

    def nrm(shape, fan_in):
        return jax.random.normal(ks.pop(), shape, f32) * (fan_in ** -0.5)

    def gain(shape):
        return 1.0 + 0.02 * jax.random.normal(ks.pop(), shape, f32)

    def bias(shape):
        return 0.01 * jax.random.normal(ks.pop(), shape, f32)

    x = jax.random.normal(ks.pop(), (BATCH, SEQ, D_MODEL), f32)
    offset = jax.random.randint(ks.pop(), (BATCH, 1), 0, 1024, dtype=jnp.int32)
    positions = offset + jnp.arange(SEQ, dtype=jnp.int32)[None, :]
    u = jax.random.uniform(ks.pop(), (N_ODD, LRU_WIDTH), f32, minval=0.9, maxval=0.999)
    sg = u ** (1.0 / LRU_C)
    lru_lambda = jnp.log(sg) - jnp.log1p(-sg)
    blk = LRU_WIDTH // LRU_BLOCKS
    return {
        'x': x,
        'positions': positions,
        'mixer_norm_g': gain((DEPTH, D_MODEL)),
        'mlp_norm_g': gain((DEPTH, D_MODEL)),
        'mlp_w1': nrm((DEPTH, D_MODEL, D_FF), D_MODEL),
        'mlp_w2': nrm((DEPTH, D_FF, D_MODEL), D_FF),
        'final_norm_g': gain((D_MODEL,)),
        'ev_w_in': nrm((N_EVEN, D_MODEL, EVEN_IN), D_MODEL),
        'ev_conv_w': nrm((N_EVEN, SC_KERNEL, SC_WIDTH), SC_KERNEL),
        'mla_q_norm_g': gain((N_EVEN, MLA_Q_RANK)),
        'mla_w_uq': nrm((N_EVEN, MLA_Q_RANK, MLA_HEADS * (MLA_NOPE + MLA_ROPE)), MLA_Q_RANK),
        'mla_kv_norm_g': gain((N_EVEN, MLA_KV_RANK)),
        'mla_w_ukv': nrm((N_EVEN, MLA_KV_RANK, MLA_HEADS * (MLA_NOPE + MLA_V)), MLA_KV_RANK),
        'ev_w_out': nrm((N_EVEN, EVEN_MIX, D_MODEL), EVEN_MIX),
        'od_w_in': nrm((N_ODD, D_MODEL, ODD_IN), D_MODEL),
        'gla_w_gate_up': nrm((N_ODD, GLA_GATE_RANK, GLA_HEADS * GLA_DK), GLA_GATE_RANK),
        'gla_b_gate': bias((N_ODD, GLA_HEADS * GLA_DK)),
        'gla_norm_g': gain((N_ODD, GLA_DV)),
        'lru_conv_w': nrm((N_ODD, LRU_KERNEL, LRU_WIDTH), LRU_KERNEL),
        'lru_conv_b': bias((N_ODD, LRU_WIDTH)),
        'lru_w_a': nrm((N_ODD, LRU_BLOCKS, blk, blk), blk),
        'lru_b_a': bias((N_ODD, LRU_WIDTH)),
        'lru_w_i': nrm((N_ODD, LRU_BLOCKS, blk, blk), blk),
        'lru_b_i': bias((N_ODD, LRU_WIDTH)),
        'lru_lambda': lru_lambda,
        'od_w_out': nrm((N_ODD, ODD_MIX, D_MODEL), ODD_MIX),
    }


def reference(x, positions, mixer_norm_g, mlp_norm_g, mlp_w1, mlp_w2, final_norm_g,
              ev_w_in, ev_conv_w, mla_q_norm_g, mla_w_uq, mla_kv_norm_g, mla_w_ukv, ev_w_out,
              od_w_in, gla_w_gate_up, gla_b_gate, gla_norm_g, lru_conv_w, lru_conv_b,
              lru_w_a, lru_b_a, lru_w_i, lru_b_i, lru_lambda, od_w_out):
    h = x
    for layer in range(DEPTH):
        j = layer // 2
        hn = rmsnorm(h, mixer_norm_g[layer])
        if layer % 2 == 0:
            mix = even_mixer(hn, positions, ev_w_in[j], ev_conv_w[j], mla_q_norm_g[j], mla_w_uq[j],
                             mla_kv_norm_g[j], mla_w_ukv[j], ev_w_out[j])
        else:
            mix = odd_mixer(hn, od_w_in[j], gla_w_gate_up[j], gla_b_gate[j], gla_norm_g[j],
                            lru_conv_w[j], lru_conv_b[j], lru_w_a[j], lru_b_a[j], lru_w_i[j],
                            lru_b_i[j], lru_lambda[j], od_w_out[j])
        h = h + mix.astype(h.dtype)
        hn = rmsnorm(h, mlp_norm_g[layer])
        h = h + (jnp.square(jax.nn.relu(hn @ mlp_w1[layer])) @ mlp_w2[layer]).astype(h.dtype)
    return rmsnorm(h, final_norm_g)
```

```python
import functools

import jax
import jax.numpy as jnp
from jax import lax
from jax.experimental import pallas as pl
from jax.experimental.pallas import tpu as pltpu

F32 = jnp.float32
BF16 = jnp.bfloat16

D_MODEL = 1024
D_FF = 4 * D_MODEL
NORM_EPS = 1e-6

SC_WIDTH = 512
SC_KERNEL = 3
MLA_HEADS = 8
MLA_NOPE = 64
MLA_ROPE = 32
MLA_V = 64
MLA_Q_RANK = 384
MLA_KV_RANK = 256
ROPE_THETA = 10000.0

GLA_HEADS = 4
GLA_DK = 64
GLA_DV = 128
GLA_GATE_RANK = 16
GLA_TAU = 16.0
GLA_CHUNK = 64
LRU_WIDTH = 512
LRU_BLOCKS = 8
LRU_KERNEL = 4
LRU_C = 8.0

LANES = 128
SUBLANES = 8
VMEM_LIMIT_BYTES = 56 * 1024 * 1024

HEAD_PAD = LANES
MLA_PAD = MLA_HEADS * HEAD_PAD

EV_GB, EV_GC, EV_U = 0, 512, 1024
EV_CQ = 1536
EV_CKV = EV_CQ + MLA_Q_RANK
EV_KR = EV_CKV + MLA_KV_RANK
EV_KRS = EV_KR + LANES
EV_IN_PAD = EV_KRS + LANES

OD_V, OD_OG, OD_X, OD_Y = 0, 512, 1024, 1536
OD_Q, OD_K, OD_G = 2048, 2304, 2560
OD_IN_PAD = OD_G + LANES

ROW_BLOCK = 512
FF_BLOCK = 1024
ATTN_BLOCK = 512
GLA_BLOCK = 256
HALO = SUBLANES


def _rms(x):
    return x * lax.rsqrt(jnp.mean(x * x, axis=-1, keepdims=True) + NORM_EPS)


def _dot(a, b):
    return jnp.dot(a, b, preferred_element_type=F32)


def _dot_nt(a, b):
    return lax.dot_general(a, b, (((1,), (1,)), ((), ())), preferred_element_type=F32)


def _dot_tn(a, b):
    return lax.dot_general(a, b, (((0,), (0,)), ((), ())), preferred_element_type=F32)


def _sigmoid(x):
    return 1.0 / (1.0 + jnp.exp(-x))


def _softplus(x):
    return jnp.maximum(x, 0.0) + jnp.log1p(jnp.exp(-jnp.abs(x)))


def _params(*semantics):
    return pltpu.CompilerParams(dimension_semantics=semantics, vmem_limit_bytes=VMEM_LIMIT_BYTES)


def _causal_conv(ext_ref, x, w_ref, first_block):
    rows = x.shape[0]
    taps = w_ref.shape[0]

    @pl.when(first_block)
    def _():
        ext_ref[0:HALO, :] = jnp.zeros((HALO, x.shape[1]), F32)

    ext_ref[HALO:HALO + rows, :] = x
    out = None
    for k in range(taps):
        start = HALO - (taps - 1) + k
        term = w_ref[k:k + 1, :] * ext_ref[start:start + rows, :]
        out = term if out is None else out + term
    ext_ref[0:HALO, :] = ext_ref[rows:rows + HALO, :]
    return out


def _rope_table_kernel(pos_ref, freq_ref, sign_ref, cos_ref, sin_ref):
    ang = pos_ref[...].astype(F32) * freq_ref[...]
    cos_ref[...] = jnp.cos(ang)
    sin_ref[...] = jnp.sin(ang) * sign_ref[...]


def _rope_tables(positions):
    b, s = positions.shape
    t = b * s
    inv_freq = ROPE_THETA ** (-jnp.arange(0, MLA_ROPE, 2, dtype=F32) / MLA_ROPE)
    half = MLA_ROPE // 2
    zeros_lo = jnp.zeros((MLA_NOPE,), F32)
    zeros_hi = jnp.zeros((HEAD_PAD - MLA_NOPE - MLA_ROPE,), F32)
    freq = jnp.concatenate([zeros_lo, inv_freq, inv_freq, zeros_hi])[None, :]
    sign = jnp.concatenate([zeros_lo, -jnp.ones((half,), F32), jnp.ones((half,), F32), zeros_hi])[None, :]
    ts = ROW_BLOCK
    row = pl.BlockSpec((ts, 1), lambda i: (i, 0))
    lane = pl.BlockSpec((1, HEAD_PAD), lambda i: (0, 0))
    out = pl.BlockSpec((ts, HEAD_PAD), lambda i: (i, 0))
    return pl.pallas_call(
        _rope_table_kernel,
        grid=(t // ts,),
        in_specs=[row, lane, lane],
        out_specs=[out, out],
        out_shape=[jax.ShapeDtypeStruct((t, HEAD_PAD), F32)] * 2,
        compiler_params=_params("parallel"),
        name="rope_tables",
    )(positions.reshape(t, 1), freq, sign)


def _even_pre_kernel(h_ref, g_ref, win_ref, cw_ref, qg_ref, kvg_ref, wq_ref, wqs_ref, wk_ref, wv_ref,
                     cos_ref, sin_ref, sc_ref, q_ref, k_ref, v_ref, ext_ref):
    hn = (_rms(h_ref[...]) * g_ref[...]).astype(BF16)
    proj = _dot(hn, win_ref[...])

    gate_b = proj[:, EV_GB:EV_GB + SC_WIDTH]
    cu = proj[:, EV_GC:EV_GC + SC_WIDTH] * proj[:, EV_U:EV_U + SC_WIDTH]
    conv = _causal_conv(ext_ref, cu, cw_ref, pl.program_id(1) == 0)
    sc_ref[...] = (gate_b * conv).astype(BF16)

    cos = cos_ref[...]
    sin = sin_ref[...]
    cqn = (_rms(proj[:, EV_CQ:EV_CQ + MLA_Q_RANK]) * qg_ref[...]).astype(BF16)
    q0 = _dot(cqn, wq_ref[...])
    q1 = _dot(cqn, wqs_ref[...])
    ckvn = (_rms(proj[:, EV_CKV:EV_CKV + MLA_KV_RANK]) * kvg_ref[...]).astype(BF16)
    kn = _dot(ckvn, wk_ref[...])
    v_ref[...] = _dot(ckvn, wv_ref[...]).astype(BF16)
    k_rope = proj[:, EV_KR:EV_KR + LANES] * cos + proj[:, EV_KRS:EV_KRS + LANES] * sin
    for head in range(MLA_HEADS):
        sl = slice(head * HEAD_PAD, (head + 1) * HEAD_PAD)
        q_ref[:, sl] = (q0[:, sl] * cos + q1[:, sl] * sin).astype(BF16)
        k_ref[:, sl] = (kn[:, sl] + k_rope).astype(BF16)


def _even_pre(h, norm_g, w, cos, sin, batch, seq):
    t = batch * seq
    tm = ROW_BLOCK
    nblk = seq // tm
    rows = lambda width: pl.BlockSpec((tm, width), lambda b, i: (b * nblk + i, 0))
    full = lambda arr: pl.BlockSpec(arr.shape, lambda b, i: (0,) * arr.ndim)
    consts = [norm_g, w["w_in"], w["conv_w"], w["q_norm_g"], w["kv_norm_g"], w["w_q"], w["w_q_swap"], w["w_k"],
              w["w_v"]]
    return pl.pallas_call(
        _even_pre_kernel,
        grid=(batch, nblk),
        in_specs=[rows(D_MODEL)] + [full(a) for a in consts] + [rows(HEAD_PAD), rows(HEAD_PAD)],
        out_specs=[rows(SC_WIDTH), rows(MLA_PAD), rows(MLA_PAD), rows(MLA_PAD)],
        out_shape=[jax.ShapeDtypeStruct((t, SC_WIDTH), BF16)] + [jax.ShapeDtypeStruct((t, MLA_PAD), BF16)] * 3,
        scratch_shapes=[pltpu.VMEM((tm + HALO, SC_WIDTH), F32)],
        compiler_params=_params("parallel", "arbitrary"),
        name="even_pre",
    )(h, *consts, cos, sin)


def _attn_kernel(q_ref, k_ref, v_ref, o_ref):
    tq = q_ref.shape[0]
    qi = pl.program_id(1)
    scale = (MLA_NOPE + MLA_ROPE) ** -0.5
    row = lax.broadcasted_iota(jnp.int32, (tq, tq), 0)
    col = lax.broadcasted_iota(jnp.int32, (tq, tq), 1)
    causal = col <= row

    def head_out(head):
        sl = slice(head * HEAD_PAD, (head + 1) * HEAD_PAD)
        q = q_ref[:, sl]

        def step(j, carry, masked):
            m, l, acc = carry
            start = pl.multiple_of(j * tq, tq)
            s = _dot_nt(q, k_ref[pl.ds(start, tq), sl]) * scale
            if masked:
                s = jnp.where(causal, s, -jnp.inf)
            m_new = jnp.maximum(m, jnp.max(s, axis=-1, keepdims=True))
            p = jnp.exp(s - m_new)
            alpha = jnp.exp(m - m_new)
            l = alpha * l + jnp.sum(p, axis=-1, keepdims=True)
            acc = alpha * acc + _dot(p.astype(BF16), v_ref[pl.ds(start, tq), sl])
            return m_new, l, acc

        init = (jnp.full((tq, 1), -jnp.inf, F32), jnp.zeros((tq, 1), F32), jnp.zeros((tq, HEAD_PAD), F32))
        carry = lax.fori_loop(0, qi, lambda j, c: step(j, c, False), init)
        _, l, acc = step(qi, carry, True)
        return acc / l

    for pair in range(MLA_HEADS // 2):
        o_ref[:, pair * LANES:(pair + 1) * LANES] = (head_out(2 * pair) + head_out(2 * pair + 1)).astype(BF16)


def _attention(q, k, v, batch, seq):
    tq = ATTN_BLOCK
    q3, k3, v3 = (a.reshape(batch, seq, MLA_PAD) for a in (q, k, v))
    out = pl.pallas_call(
        _attn_kernel,
        grid=(batch, seq // tq),
        in_specs=[pl.BlockSpec((None, tq, MLA_PAD), lambda b, i: (b, i, 0)),
                  pl.BlockSpec((None, seq, MLA_PAD), lambda b, i: (b, 0, 0)),
                  pl.BlockSpec((None, seq, MLA_PAD), lambda b, i: (b, 0, 0))],
        out_specs=pl.BlockSpec((None, tq, MLA_HEADS * MLA_V), lambda b, i: (b, i, 0)),
        out_shape=jax.ShapeDtypeStruct((batch, seq, MLA_HEADS * MLA_V), BF16),
        compiler_params=_params("parallel", "arbitrary"),
        name="mla_attention",
    )(q3, k3, v3)
    return out.reshape(batch * seq, MLA_HEADS * MLA_V)


def _norm_matmul_kernel(h_ref, g_ref, w_ref, o_ref):
    hn = (_rms(h_ref[...]) * g_ref[...]).astype(BF16)
    o_ref[...] = _dot(hn, w_ref[...])


def _norm_matmul(h, norm_g, w):
    t = h.shape[0]
    tm = ROW_BLOCK
    n = w.shape[1]
    return pl.pallas_call(
        _norm_matmul_kernel,
        grid=(t // tm,),
        in_specs=[pl.BlockSpec((tm, D_MODEL), lambda i: (i, 0)),
                  pl.BlockSpec((1, D_MODEL), lambda i: (0, 0)),
                  pl.BlockSpec(w.shape, lambda i: (0, 0))],
        out_specs=pl.BlockSpec((tm, n), lambda i: (i, 0)),
        out_shape=jax.ShapeDtypeStruct((t, n), F32),
        compiler_params=_params("parallel"),
        name="norm_matmul",
    )(h, norm_g, w)


def _gla_kernel(q_ref, k_ref, v_ref, g_ref, og_ref, gup_ref, gbias_ref, ng_ref, o_ref, state_ref):
    tb = q_ref.shape[0]
    n_pairs = GLA_HEADS // 2
    pair_dk = 2 * GLA_DK
    pair_dv = 2 * GLA_DV

    @pl.when(pl.program_id(1) == 0)
    def _():
        state_ref[...] = jnp.zeros(state_ref.shape, F32)

    gate = _dot(g_ref[...].astype(BF16), gup_ref[...]) + gbias_ref[...]
    log_a = (jnp.minimum(gate, 0.0) - jnp.log1p(jnp.exp(-jnp.abs(gate)))) / GLA_TAU

    row = lax.broadcasted_iota(jnp.int32, (tb, tb), 0)
    col = lax.broadcasted_iota(jnp.int32, (tb, tb), 1)
    same_chunk = (row // GLA_CHUNK) == (col // GLA_CHUNK)
    tril = same_chunk & (col <= row)
    cum = jnp.dot(tril.astype(F32), log_a, preferred_element_type=F32, precision=lax.Precision.HIGHEST)
    last = jnp.dot(same_chunk.astype(F32), log_a, preferred_element_type=F32, precision=lax.Precision.HIGHEST)

    q = q_ref[...] * (GLA_DK ** -0.5)
    k = k_ref[...]
    q_in = q * jnp.exp(cum)
    k_in = k * jnp.exp(-cum)
    k_out = k * jnp.exp(last - cum)
    decay = jnp.exp(last)
    v = v_ref[...].astype(BF16)

    lane = lax.broadcasted_iota(jnp.int32, (tb, pair_dk), 1)
    state_rows = lax.broadcasted_iota(jnp.int32, (pair_dv, pair_dk), 0)
    state_lanes = lax.broadcasted_iota(jnp.int32, (pair_dv, pair_dk), 1)
    own_block = (state_rows < GLA_DV) == (state_lanes < GLA_DK)

    for pair in range(n_pairs):
        ksl = slice(pair * pair_dk, (pair + 1) * pair_dk)
        vsl = slice(pair * pair_dv, (pair + 1) * pair_dv)
        qp, kp, kop = q_in[:, ksl], k_in[:, ksl], k_out[:, ksl]
        vp = v[:, vsl]

        intra = []
        for sub in range(2):
            mine = (lane < GLA_DK) if sub == 0 else (lane >= GLA_DK)
            qh = jnp.where(mine, qp, 0.0).astype(BF16)
            kh = jnp.where(mine, kp, 0.0).astype(BF16)
            att = jnp.where(tril, _dot_nt(qh, kh), 0.0)
            intra.append(_dot(att.astype(BF16), vp[:, sub * GLA_DV:(sub + 1) * GLA_DV]))

        inter = []
        for c in range(tb // GLA_CHUNK):
            rows = slice(c * GLA_CHUNK, (c + 1) * GLA_CHUNK)
            state = state_ref[pair]
            inter.append(_dot_nt(qp[rows].astype(BF16), state.astype(BF16)))
            kv = jnp.where(own_block, _dot_tn(vp[rows], kop[rows].astype(BF16)), 0.0)
            state_ref[pair] = state * decay[c * GLA_CHUNK:c * GLA_CHUNK + 1, ksl] + kv
        o_inter = jnp.concatenate(inter, axis=0)

        for sub in range(2):
            head = 2 * pair + sub
            hsl = slice(head * GLA_DV, (head + 1) * GLA_DV)
            o = intra[sub] + o_inter[:, sub * GLA_DV:(sub + 1) * GLA_DV]
            o = _rms(o) * ng_ref[...]
            og = og_ref[:, hsl]
            o_ref[:, hsl] = (o * (og * _sigmoid(og))).astype(BF16)


def _gla(proj, w, batch, seq):
    t = batch * seq
    tb = GLA_BLOCK
    nblk = seq // tb
    cols = lambda width, offset: pl.BlockSpec((tb, width), lambda b, i: (b * nblk + i, offset // width))
    full = lambda arr: pl.BlockSpec(arr.shape, lambda b, i: (0,) * arr.ndim)
    consts = [w["gate_up"], w["gate_bias"], w["gla_norm_g"]]
    width_v = GLA_HEADS * GLA_DV
    width_k = GLA_HEADS * GLA_DK
    return pl.pallas_call(
        _gla_kernel,
        grid=(batch, nblk),
        in_specs=[cols(width_k, OD_Q), cols(width_k, OD_K), cols(width_v, OD_V), cols(LANES, OD_G),
                  cols(width_v, OD_OG)] + [full(a) for a in consts],
        out_specs=pl.BlockSpec((tb, width_v), lambda b, i: (b * nblk + i, 0)),
        out_shape=jax.ShapeDtypeStruct((t, width_v), BF16),
        scratch_shapes=[pltpu.VMEM((GLA_HEADS // 2, 2 * GLA_DV, 2 * GLA_DK), F32)],
        compiler_params=_params("parallel", "arbitrary"),
        name="gla",
    )(proj, proj, proj, proj, proj, *consts)


def _lru_kernel(x_ref, y_ref, cw_ref, cb_ref, wa_ref, ba_ref, wi_ref, bi_ref, lam_ref, o_ref,
                ext_ref, carry_ref, a_ref, b_ref):
    tb = x_ref.shape[0]
    first = pl.program_id(1) == 0

    @pl.when(first)
    def _():
        carry_ref[...] = jnp.zeros(carry_ref.shape, F32)

    xc = _causal_conv(ext_ref, x_ref[...], cw_ref, first) + cb_ref[...]
    xcb = xc.astype(BF16)
    r = _sigmoid(_dot(xcb, wa_ref[...]) + ba_ref[...])
    gate_i = _sigmoid(_dot(xcb, wi_ref[...]) + bi_ref[...])
    log_a = (-LRU_C * r) * _softplus(-lam_ref[...])
    a = jnp.exp(log_a)
    b = jnp.sqrt(-jnp.tanh(log_a) * (a * a + 1.0)) * (gate_i * xc)

    sub = lax.broadcasted_iota(jnp.int32, (tb, LRU_WIDTH), 0) % SUBLANES
    shift = 1
    while shift < SUBLANES:
        a_prev = pltpu.roll(a, shift, axis=0)
        b_prev = pltpu.roll(b, shift, axis=0)
        take = sub >= shift
        b = jnp.where(take, a * b_prev + b, b)
        a = jnp.where(take, a * a_prev, a)
        shift *= 2
    a_ref[...] = a.reshape(tb // SUBLANES, SUBLANES, LRU_WIDTH)
    b_ref[...] = b.reshape(tb // SUBLANES, SUBLANES, LRU_WIDTH)

    def chain(g, h_prev):
        h = a_ref[g] * h_prev + b_ref[g]
        b_ref[g] = h
        return jnp.broadcast_to(h[SUBLANES - 1:SUBLANES, :], (SUBLANES, LRU_WIDTH))

    carry_ref[...] = lax.fori_loop(0, tb // SUBLANES, chain, carry_ref[...])
    hseq = b_ref[...].reshape(tb, LRU_WIDTH)

    y = y_ref[...]
    gelu = y * (0.5 * (1.0 + jnp.tanh(0.7978845608028654 * (y + 0.044715 * (y * y * y)))))
    o_ref[...] = (gelu * hseq).astype(BF16)


def _lru(proj, w, batch, seq):
    t = batch * seq
    tb = ROW_BLOCK
    nblk = seq // tb
    cols = lambda offset: pl.BlockSpec((tb, LRU_WIDTH), lambda b, i: (b * nblk + i, offset // LRU_WIDTH))
    full = lambda arr: pl.BlockSpec(arr.shape, lambda b, i: (0,) * arr.ndim)
    consts = [w["conv_w"], w["conv_b"], w["w_a"], w["b_a"], w["w_i"], w["b_i"], w["lam"]]
    return pl.pallas_call(
        _lru_kernel,
        grid=(batch, nblk),
        in_specs=[cols(OD_X), cols(OD_Y)] + [full(a) for a in consts],
        out_specs=pl.BlockSpec((tb, LRU_WIDTH), lambda b, i: (b * nblk + i, 0)),
        out_shape=jax.ShapeDtypeStruct((t, LRU_WIDTH), BF16),
        scratch_shapes=[pltpu.VMEM((tb + HALO, LRU_WIDTH), F32),
                        pltpu.VMEM((SUBLANES, LRU_WIDTH), F32),
                        pltpu.VMEM((tb // SUBLANES, SUBLANES, LRU_WIDTH), F32),
                        pltpu.VMEM((tb // SUBLANES, SUBLANES, LRU_WIDTH), F32)],
        compiler_params=_params("parallel", "arbitrary"),
        name="lru",
    )(proj, proj, *consts)


def _post_kernel(h_ref, a_ref, b_ref, wo_ref, g_ref, w1_ref, w2_ref, gf_ref, o_ref, h1_ref, hn_ref, acc_ref,
                 *, final_norm):
    j = pl.program_id(1)
    half = a_ref.shape[1]

    @pl.when(j == 0)
    def _():
        mix = _dot(a_ref[...], wo_ref[0:half, :]) + _dot(b_ref[...], wo_ref[half:2 * half, :])
        h1 = h_ref[...] + mix
        h1_ref[...] = h1
        hn_ref[...] = (_rms(h1) * g_ref[...]).astype(BF16)
        acc_ref[...] = jnp.zeros(acc_ref.shape, F32)

    hidden = _dot(hn_ref[...], w1_ref[...])
    act = jnp.square(jnp.maximum(hidden, 0.0)).astype(BF16)
    acc_ref[...] += _dot(act, w2_ref[...])

    @pl.when(j == pl.num_programs(1) - 1)
    def _():
        out = h1_ref[...] + acc_ref[...]
        if final_norm:
            out = _rms(out) * gf_ref[...]
        o_ref[...] = out


def _post(h, a, b, w_out, norm_g, w1, w2, final_g, final_norm):
    t = h.shape[0]
    tm = ROW_BLOCK
    tf = FF_BLOCK
    rows = lambda width: pl.BlockSpec((tm, width), lambda i, j: (i, 0))
    vec = pl.BlockSpec((1, D_MODEL), lambda i, j: (0, 0))
    return pl.pallas_call(
        functools.partial(_post_kernel, final_norm=final_norm),
        grid=(t // tm, D_FF // tf),
        in_specs=[rows(D_MODEL), rows(a.shape[1]), rows(b.shape[1]),
                  pl.BlockSpec(w_out.shape, lambda i, j: (0, 0)), vec,
                  pl.BlockSpec((D_MODEL, tf), lambda i, j: (0, j)),
                  pl.BlockSpec((tf, D_MODEL), lambda i, j: (j, 0)), vec],
        out_specs=rows(D_MODEL),
        out_shape=jax.ShapeDtypeStruct((t, D_MODEL), F32),
        scratch_shapes=[pltpu.VMEM((tm, D_MODEL), F32), pltpu.VMEM((tm, D_MODEL), BF16),
                        pltpu.VMEM((tm, D_MODEL), F32)],
        compiler_params=_params("parallel", "arbitrary"),
        name="post_mlp",
    )(h, a, b, w_out, norm_g, w1, w2, final_g)


def _even_weights(w_in, conv_w, q_norm_g, w_uq, kv_norm_g, w_ukv):
    half = MLA_ROPE // 2
    z = lambda *shape: jnp.zeros(shape, F32)
    kr = w_in[:, EV_KR:EV_KR + MLA_ROPE]
    kr1, kr2 = kr[:, :half], kr[:, half:]
    tail = HEAD_PAD - MLA_NOPE - MLA_ROPE
    kr_blk = jnp.concatenate([z(D_MODEL, MLA_NOPE), kr1, kr2, z(D_MODEL, tail)], axis=1)
    krs_blk = jnp.concatenate([z(D_MODEL, MLA_NOPE), kr2, kr1, z(D_MODEL, tail)], axis=1)
    w_in_pad = jnp.concatenate([w_in[:, :EV_KR], kr_blk, krs_blk], axis=1)

    wq = w_uq.reshape(MLA_Q_RANK, MLA_HEADS, MLA_NOPE + MLA_ROPE)
    nope, r1, r2 = wq[..., :MLA_NOPE], wq[..., MLA_NOPE:MLA_NOPE + half], wq[..., MLA_NOPE + half:]
    zq = z(MLA_Q_RANK, MLA_HEADS, tail)
    w_q = jnp.concatenate([nope, r1, r2, zq], axis=-1).reshape(MLA_Q_RANK, MLA_PAD)
    w_q_swap = jnp.concatenate([jnp.zeros_like(nope), r2, r1, zq], axis=-1).reshape(MLA_Q_RANK, MLA_PAD)

    wkv = w_ukv.reshape(MLA_KV_RANK, MLA_HEADS, MLA_NOPE + MLA_V)
    k_nope, v = wkv[..., :MLA_NOPE], wkv[..., MLA_NOPE:]
    w_k = jnp.concatenate([k_nope, z(MLA_KV_RANK, MLA_HEADS, HEAD_PAD - MLA_NOPE)], axis=-1)
    zv = jnp.zeros_like(v)
    odd_head = (jnp.arange(MLA_HEADS) % 2 == 1)[None, :, None]
    w_v = jnp.concatenate([jnp.where(odd_head, zv, v), jnp.where(odd_head, v, zv)], axis=-1)
    return {
        "w_in": w_in_pad.astype(BF16), "conv_w": conv_w, "q_norm_g": q_norm_g[None, :],
        "kv_norm_g": kv_norm_g[None, :], "w_q": w_q.astype(BF16), "w_q_swap": w_q_swap.astype(BF16),
        "w_k": w_k.reshape(MLA_KV_RANK, MLA_PAD).astype(BF16), "w_v": w_v.reshape(MLA_KV_RANK, MLA_PAD).astype(BF16),
    }


def _block_diag(w):
    n, r, c = w.shape
    eye = jnp.eye(n, dtype=w.dtype)
    return (eye[:, None, :, None] * w[:, :, None, :]).reshape(n * r, n * c)


def _odd_weights(w_in, gate_up, gate_bias, gla_norm_g, conv_w, conv_b, w_a, b_a, w_i, b_i, lam):
    dk, dv = GLA_HEADS * GLA_DK, GLA_HEADS * GLA_DV
    o = 0
    parts = {}
    for name, width in (("q", dk), ("k", dk), ("v", dv), ("g", GLA_GATE_RANK), ("og", dv), ("x", LRU_WIDTH),
                        ("y", LRU_WIDTH)):
        parts[name] = w_in[:, o:o + width]
        o += width
    g_pad = jnp.zeros((D_MODEL, LANES - GLA_GATE_RANK), F32)
    w_in_pad = jnp.concatenate([parts["v"], parts["og"], parts["x"], parts["y"], parts["q"], parts["k"], parts["g"],
                                g_pad], axis=1)
    gate_up_pad = jnp.concatenate([gate_up, jnp.zeros((LANES - GLA_GATE_RANK, dk), F32)], axis=0)
    return {
        "w_in": w_in_pad.astype(BF16), "gate_up": gate_up_pad.astype(BF16), "gate_bias": gate_bias[None, :],
        "gla_norm_g": gla_norm_g[None, :], "conv_w": conv_w, "conv_b": conv_b[None, :],
        "w_a": _block_diag(w_a).astype(BF16), "b_a": b_a[None, :], "w_i": _block_diag(w_i).astype(BF16),
        "b_i": b_i[None, :], "lam": lam[None, :],
    }


def kernel(x, positions, mixer_norm_g, mlp_norm_g, mlp_w1, mlp_w2, final_norm_g, ev_w_in, ev_conv_w, mla_q_norm_g, mla_w_uq, mla_kv_norm_g, mla_w_ukv, ev_w_out, od_w_in, gla_w_gate_up, gla_b_gate, gla_norm_g, lru_conv_w, lru_conv_b, lru_w_a, lru_b_a, lru_w_i, lru_b_i, lru_lambda, od_w_out):
    batch, seq, d_model = x.shape
    depth = mixer_norm_g.shape[0]
    assert d_model == D_MODEL and seq % ROW_BLOCK == 0 and seq % ATTN_BLOCK == 0 and seq % GLA_BLOCK == 0
    h = x.reshape(batch * seq, D_MODEL)
    cos, sin = _rope_tables(positions)
    final_g = final_norm_g[None, :]
    for layer in range(depth):
        j = layer // 2
        norm_g = mixer_norm_g[layer][None, :]
        if layer % 2 == 0:
            w = _even_weights(ev_w_in[j], ev_conv_w[j], mla_q_norm_g[j], mla_w_uq[j], mla_kv_norm_g[j], mla_w_ukv[j])
            mix_a, q, k, v = _even_pre(h, norm_g, w, cos, sin, batch, seq)
            mix_b = _attention(q, k, v, batch, seq)
            w_out = ev_w_out[j]
        else:
            w = _odd_weights(od_w_in[j], gla_w_gate_up[j], gla_b_gate[j], gla_norm_g[j], lru_conv_w[j],
                             lru_conv_b[j], lru_w_a[j], lru_b_a[j], lru_w_i[j], lru_b_i[j], lru_lambda[j])
            proj = _norm_matmul(h, norm_g, w["w_in"])
            mix_a = _gla(proj, w, batch, seq)
            mix_b = _lru(proj, w, batch, seq)
            w_out = od_w_out[j]
        h = _post(h, mix_a, mix_b, w_out.astype(BF16), mlp_norm_g[layer][None, :], mlp_w1[layer].astype(BF16),
                  mlp_w2[layer].astype(BF16), final_g, final_norm=(layer == depth - 1))
    return h.reshape(batch, seq, D_MODEL)
```

```python
import functools

import jax
import jax.numpy as jnp
from jax import lax
from jax.experimental import pallas as pl
from jax.experimental.pallas import tpu as pltpu

F32 = jnp.float32
BF16 = jnp.bfloat16

D_MODEL = 1024
D_FF = 4 * D_MODEL
NORM_EPS = 1e-6

SC_WIDTH = 512
SC_KERNEL = 3
MLA_HEADS = 8
MLA_NOPE = 64
MLA_ROPE = 32
MLA_V = 64
MLA_Q_RANK = 384
MLA_KV_RANK = 256
ROPE_THETA = 10000.0
QK_SCALE_LOG2E = (MLA_NOPE + MLA_ROPE) ** -0.5 * 1.4426950408889634

GLA_HEADS = 4
GLA_DK = 64
GLA_DV = 128
GLA_GATE_RANK = 16
GLA_TAU = 16.0
GLA_CHUNK = 64
LRU_WIDTH = 512
LRU_BLOCKS = 8
LRU_KERNEL = 4
LRU_C = 8.0

LANES = 128
SUBLANES = 8
VMEM_LIMIT_BYTES = 56 * 1024 * 1024

HEAD_PAD = LANES
MLA_PAD = MLA_HEADS * HEAD_PAD

EV_GB, EV_GC, EV_U = 0, 512, 1024
EV_CQ = 1536
EV_CKV = EV_CQ + MLA_Q_RANK
EV_KR = EV_CKV + MLA_KV_RANK
EV_KRS = EV_KR + LANES
EV_IN_PAD = EV_KRS + LANES

OD_V, OD_OG, OD_X, OD_Y = 0, 512, 1024, 1536
OD_Q, OD_K, OD_G = 2048, 2304, 2560
OD_IN_PAD = OD_G + LANES

ROW_BLOCK = 512
FF_BLOCK = 1024
POST_ROW_SPLIT = 2
ATTN_BLOCK = 512
ATTN_HEAD_GROUP = 8
GLA_BLOCK = 256
HALO = SUBLANES


def _rms(x):
    return x * lax.rsqrt(jnp.mean(x * x, axis=-1, keepdims=True) + NORM_EPS)


def _dot(a, b):
    return jnp.dot(a, b, preferred_element_type=F32)


def _dot_nt(a, b):
    return lax.dot_general(a, b, (((1,), (1,)), ((), ())), preferred_element_type=F32)


def _dot_tn(a, b):
    return lax.dot_general(a, b, (((0,), (0,)), ((), ())), preferred_element_type=F32)


def _sigmoid(x):
    return 1.0 / (1.0 + jnp.exp(-x))


def _softplus(x):
    return jnp.maximum(x, 0.0) + jnp.log1p(jnp.exp(-jnp.abs(x)))


def _params(*semantics, flags=None):
    return pltpu.CompilerParams(dimension_semantics=semantics, vmem_limit_bytes=VMEM_LIMIT_BYTES, flags=flags)


def _causal_conv(ext_ref, x, w_ref, first_block):
    rows = x.shape[0]
    taps = w_ref.shape[0]

    @pl.when(first_block)
    def _():
        ext_ref[0:HALO, :] = jnp.zeros((HALO, x.shape[1]), F32)

    ext_ref[HALO:HALO + rows, :] = x
    out = None
    for k in range(taps):
        start = HALO - (taps - 1) + k
        term = w_ref[k:k + 1, :] * ext_ref[start:start + rows, :]
        out = term if out is None else out + term
    ext_ref[0:HALO, :] = ext_ref[rows:rows + HALO, :]
    return out


def _rope_table_kernel(pos_ref, freq_ref, sign_ref, cos_ref, sin_ref):
    ang = pos_ref[...].astype(F32) * freq_ref[...]
    cos_ref[...] = jnp.cos(ang)
    sin_ref[...] = jnp.sin(ang) * sign_ref[...]


def _rope_tables(positions):
    b, s = positions.shape
    t = b * s
    inv_freq = ROPE_THETA ** (-jnp.arange(0, MLA_ROPE, 2, dtype=F32) / MLA_ROPE)
    half = MLA_ROPE // 2
    zeros_lo = jnp.zeros((MLA_NOPE,), F32)
    zeros_hi = jnp.zeros((HEAD_PAD - MLA_NOPE - MLA_ROPE,), F32)
    freq = jnp.concatenate([zeros_lo, inv_freq, inv_freq, zeros_hi])[None, :]
    sign = jnp.concatenate([zeros_lo, -jnp.ones((half,), F32), jnp.ones((half,), F32), zeros_hi])[None, :]
    ts = ROW_BLOCK
    row = pl.BlockSpec((ts, 1), lambda i: (i, 0))
    lane = pl.BlockSpec((1, HEAD_PAD), lambda i: (0, 0))
    out = pl.BlockSpec((ts, HEAD_PAD), lambda i: (i, 0))
    return pl.pallas_call(
        _rope_table_kernel,
        grid=(t // ts,),
        in_specs=[row, lane, lane],
        out_specs=[out, out],
        out_shape=[jax.ShapeDtypeStruct((t, HEAD_PAD), F32)] * 2,
        compiler_params=_params("parallel"),
        name="rope_tables",
    )(positions.reshape(t, 1), freq, sign)


def _even_pre_kernel(h_ref, g_ref, win_ref, cw_ref, qg_ref, kvg_ref, wq_ref, wqs_ref, wk_ref, wv_ref,
                     cos_ref, sin_ref, sc_ref, q_ref, k_ref, v_ref, ext_ref):
    hn = (_rms(h_ref[...]) * g_ref[...]).astype(BF16)
    proj = _dot(hn, win_ref[...])

    gate_b = proj[:, EV_GB:EV_GB + SC_WIDTH]
    cu = proj[:, EV_GC:EV_GC + SC_WIDTH] * proj[:, EV_U:EV_U + SC_WIDTH]
    conv = _causal_conv(ext_ref, cu, cw_ref, pl.program_id(1) == 0)
    sc_ref[...] = (gate_b * conv).astype(BF16)

    cos = cos_ref[...]
    sin = sin_ref[...]
    cqn = (_rms(proj[:, EV_CQ:EV_CQ + MLA_Q_RANK]) * qg_ref[...]).astype(BF16)
    q0 = _dot(cqn, wq_ref[...])
    q1 = _dot(cqn, wqs_ref[...])
    ckvn = (_rms(proj[:, EV_CKV:EV_CKV + MLA_KV_RANK]) * kvg_ref[...]).astype(BF16)
    kn = _dot(ckvn, wk_ref[...])
    vv = _dot(ckvn, wv_ref[...])
    col = lax.broadcasted_iota(jnp.int32, vv.shape, 1)
    ones_lane = (1 - (col // HEAD_PAD) % 2) * MLA_V
    v_ref[...] = jnp.where(col % HEAD_PAD == ones_lane, 1.0, vv).astype(BF16)
    k_rope = proj[:, EV_KR:EV_KR + LANES] * cos + proj[:, EV_KRS:EV_KRS + LANES] * sin
    for head in range(MLA_HEADS):
        sl = slice(head * HEAD_PAD, (head + 1) * HEAD_PAD)
        q_ref[:, sl] = ((q0[:, sl] * cos + q1[:, sl] * sin) * QK_SCALE_LOG2E).astype(BF16)
        k_ref[:, sl] = (kn[:, sl] + k_rope).astype(BF16)


def _even_pre(h, norm_g, w, cos, sin, batch, seq):
    t = batch * seq
    tm = ROW_BLOCK
    nblk = seq // tm
    rows = lambda width: pl.BlockSpec((tm, width), lambda b, i: (b * nblk + i, 0))
    full = lambda arr: pl.BlockSpec(arr.shape, lambda b, i: (0,) * arr.ndim)
    consts = [norm_g, w["w_in"], w["conv_w"], w["q_norm_g"], w["kv_norm_g"], w["w_q"], w["w_q_swap"], w["w_k"],
              w["w_v"]]
    return pl.pallas_call(
        _even_pre_kernel,
        grid=(batch, nblk),
        in_specs=[rows(D_MODEL)] + [full(a) for a in consts] + [rows(HEAD_PAD), rows(HEAD_PAD)],
        out_specs=[rows(SC_WIDTH), rows(MLA_PAD), rows(MLA_PAD), rows(MLA_PAD)],
        out_shape=[jax.ShapeDtypeStruct((t, SC_WIDTH), BF16)] + [jax.ShapeDtypeStruct((t, MLA_PAD), BF16)] * 3,
        scratch_shapes=[pltpu.VMEM((tm + HALO, SC_WIDTH), F32)],
        compiler_params=_params("parallel", "arbitrary"),
        name="even_pre",
    )(h, *consts, cos, sin)


def _attn_kernel(q_ref, k_ref, v_ref, o_ref, acc_ref):
    tq = q_ref.shape[0]
    qi = pl.program_id(1)
    row = lax.broadcasted_iota(jnp.int32, (tq, tq), 0)
    col = lax.broadcasted_iota(jnp.int32, (tq, tq), 1)
    causal = col <= row
    lane = lax.broadcasted_iota(jnp.int32, (tq, HEAD_PAD), 1)

    def update(head, j, m, masked):
        sl = slice(head * HEAD_PAD, (head + 1) * HEAD_PAD)
        start = pl.multiple_of(j * tq, tq)
        s = _dot_nt(q_ref[:, sl], k_ref[pl.ds(start, tq), sl])
        if masked:
            s = jnp.where(causal, s, -jnp.inf)
        m_new = jnp.maximum(m, jnp.max(s, axis=-1, keepdims=True))
        p = jnp.exp2(s - m_new)
        alpha = jnp.exp2(m - m_new)
        acc_ref[head] = alpha * acc_ref[head] + _dot(p.astype(BF16), v_ref[pl.ds(start, tq), sl])
        return m_new

    for group in range(MLA_HEADS // ATTN_HEAD_GROUP):
        heads = tuple(range(group * ATTN_HEAD_GROUP, (group + 1) * ATTN_HEAD_GROUP))
        for h in heads:
            acc_ref[h] = jnp.zeros((tq, HEAD_PAD), F32)
        m0 = jnp.full((tq, 1), -jnp.inf, F32)
        ms = lax.fori_loop(0, qi, lambda j, ms: tuple(update(h, j, m, False) for h, m in zip(heads, ms)),
                           (m0,) * ATTN_HEAD_GROUP)
        for h, m in zip(heads, ms):
            update(h, qi, m, True)
        for even in heads[::2]:
            acc_even = acc_ref[even]
            acc_odd = acc_ref[even + 1]
            out = jnp.where(lane < MLA_V, acc_even / acc_even[:, MLA_V:MLA_V + 1], acc_odd / acc_odd[:, 0:1])
            o_ref[:, (even // 2) * LANES:(even // 2 + 1) * LANES] = out.astype(BF16)


def _attention(q, k, v, batch, seq):
    tq = ATTN_BLOCK
    q3, k3, v3 = (a.reshape(batch, seq, MLA_PAD) for a in (q, k, v))
    out = pl.pallas_call(
        _attn_kernel,
        grid=(batch, seq // tq),
        in_specs=[pl.BlockSpec((None, tq, MLA_PAD), lambda b, i: (b, i, 0)),
                  pl.BlockSpec((None, seq, MLA_PAD), lambda b, i: (b, 0, 0)),
                  pl.BlockSpec((None, seq, MLA_PAD), lambda b, i: (b, 0, 0))],
        out_specs=pl.BlockSpec((None, tq, MLA_HEADS * MLA_V), lambda b, i: (b, i, 0)),
        out_shape=jax.ShapeDtypeStruct((batch, seq, MLA_HEADS * MLA_V), BF16),
        scratch_shapes=[pltpu.VMEM((MLA_HEADS, tq, HEAD_PAD), F32)],
        compiler_params=_params("parallel", "arbitrary"),
        name="mla_attention",
    )(q3, k3, v3)
    return out.reshape(batch * seq, MLA_HEADS * MLA_V)


def _norm_matmul_kernel(h_ref, g_ref, w_ref, o_ref):
    hn = (_rms(h_ref[...]) * g_ref[...]).astype(BF16)
    o_ref[...] = _dot(hn, w_ref[...])


def _norm_matmul(h, norm_g, w):
    t = h.shape[0]
    tm = ROW_BLOCK
    n = w.shape[1]
    return pl.pallas_call(
        _norm_matmul_kernel,
        grid=(t // tm,),
        in_specs=[pl.BlockSpec((tm, D_MODEL), lambda i: (i, 0)),
                  pl.BlockSpec((1, D_MODEL), lambda i: (0, 0)),
                  pl.BlockSpec(w.shape, lambda i: (0, 0))],
        out_specs=pl.BlockSpec((tm, n), lambda i: (i, 0)),
        out_shape=jax.ShapeDtypeStruct((t, n), F32),
        compiler_params=_params("parallel"),
        name="norm_matmul",
    )(h, norm_g, w)


def _gla_kernel(q_ref, k_ref, v_ref, g_ref, og_ref, gup_ref, gbias_ref, ng_ref, o_ref, state_ref):
    tb = q_ref.shape[0]
    n_pairs = GLA_HEADS // 2
    pair_dk = 2 * GLA_DK
    pair_dv = 2 * GLA_DV

    @pl.when(pl.program_id(1) == 0)
    def _():
        state_ref[...] = jnp.zeros(state_ref.shape, F32)

    gate = _dot(g_ref[...].astype(BF16), gup_ref[...]) + gbias_ref[...]
    log_a = (jnp.minimum(gate, 0.0) - jnp.log1p(jnp.exp(-jnp.abs(gate)))) / GLA_TAU

    row = lax.broadcasted_iota(jnp.int32, (tb, tb), 0)
    col = lax.broadcasted_iota(jnp.int32, (tb, tb), 1)
    same_chunk = (row // GLA_CHUNK) == (col // GLA_CHUNK)
    tril = same_chunk & (col <= row)
    cum = jnp.dot(tril.astype(F32), log_a, preferred_element_type=F32, precision=lax.Precision.HIGHEST)
    last = jnp.dot(same_chunk.astype(F32), log_a, preferred_element_type=F32, precision=lax.Precision.HIGHEST)

    q = q_ref[...] * (GLA_DK ** -0.5)
    k = k_ref[...]
    q_in = q * jnp.exp(cum)
    k_in = k * jnp.exp(-cum)
    k_out = k * jnp.exp(last - cum)
    decay = jnp.exp(last)
    v = v_ref[...].astype(BF16)

    lane = lax.broadcasted_iota(jnp.int32, (tb, pair_dk), 1)
    state_rows = lax.broadcasted_iota(jnp.int32, (pair_dv, pair_dk), 0)
    state_lanes = lax.broadcasted_iota(jnp.int32, (pair_dv, pair_dk), 1)
    own_block = (state_rows < GLA_DV) == (state_lanes < GLA_DK)

    for pair in range(n_pairs):
        ksl = slice(pair * pair_dk, (pair + 1) * pair_dk)
        vsl = slice(pair * pair_dv, (pair + 1) * pair_dv)
        qp, kp, kop = q_in[:, ksl], k_in[:, ksl], k_out[:, ksl]
        vp = v[:, vsl]

        intra = []
        for sub in range(2):
            mine = (lane < GLA_DK) if sub == 0 else (lane >= GLA_DK)
            qh = jnp.where(mine, qp, 0.0).astype(BF16)
            kh = jnp.where(mine, kp, 0.0).astype(BF16)
            att = jnp.where(tril, _dot_nt(qh, kh), 0.0)
            intra.append(_dot(att.astype(BF16), vp[:, sub * GLA_DV:(sub + 1) * GLA_DV]))

        inter = []
        for c in range(tb // GLA_CHUNK):
            rows = slice(c * GLA_CHUNK, (c + 1) * GLA_CHUNK)
            state = state_ref[pair]
            inter.append(_dot_nt(qp[rows].astype(BF16), state.astype(BF16)))
            kv = jnp.where(own_block, _dot_tn(vp[rows], kop[rows].astype(BF16)), 0.0)
            state_ref[pair] = state * decay[c * GLA_CHUNK:c * GLA_CHUNK + 1, ksl] + kv
        o_inter = jnp.concatenate(inter, axis=0)

        for sub in range(2):
            head = 2 * pair + sub
            hsl = slice(head * GLA_DV, (head + 1) * GLA_DV)
            o = intra[sub] + o_inter[:, sub * GLA_DV:(sub + 1) * GLA_DV]
            o = _rms(o) * ng_ref[...]
            og = og_ref[:, hsl]
            o_ref[:, hsl] = (o * (og * _sigmoid(og))).astype(BF16)


def _gla(proj, w, batch, seq):
    t = batch * seq
    tb = GLA_BLOCK
    nblk = seq // tb
    cols = lambda width, offset: pl.BlockSpec((tb, width), lambda b, i: (b * nblk + i, offset // width))
    full = lambda arr: pl.BlockSpec(arr.shape, lambda b, i: (0,) * arr.ndim)
    consts = [w["gate_up"], w["gate_bias"], w["gla_norm_g"]]
    width_v = GLA_HEADS * GLA_DV
    width_k = GLA_HEADS * GLA_DK
    return pl.pallas_call(
        _gla_kernel,
        grid=(batch, nblk),
        in_specs=[cols(width_k, OD_Q), cols(width_k, OD_K), cols(width_v, OD_V), cols(LANES, OD_G),
                  cols(width_v, OD_OG)] + [full(a) for a in consts],
        out_specs=pl.BlockSpec((tb, width_v), lambda b, i: (b * nblk + i, 0)),
        out_shape=jax.ShapeDtypeStruct((t, width_v), BF16),
        scratch_shapes=[pltpu.VMEM((GLA_HEADS // 2, 2 * GLA_DV, 2 * GLA_DK), F32)],
        compiler_params=_params("parallel", "arbitrary"),
        name="gla",
    )(proj, proj, proj, proj, proj, *consts)


def _lru_kernel(x_ref, y_ref, cw_ref, cb_ref, wa_ref, ba_ref, wi_ref, bi_ref, lam_ref, o_ref,
                ext_ref, carry_ref, a_ref, b_ref):
    tb = x_ref.shape[0]
    first = pl.program_id(1) == 0

    @pl.when(first)
    def _():
        carry_ref[...] = jnp.zeros(carry_ref.shape, F32)

    xc = _causal_conv(ext_ref, x_ref[...], cw_ref, first) + cb_ref[...]
    xcb = xc.astype(BF16)
    r = _sigmoid(_dot(xcb, wa_ref[...]) + ba_ref[...])
    gate_i = _sigmoid(_dot(xcb, wi_ref[...]) + bi_ref[...])
    log_a = (-LRU_C * r) * _softplus(-lam_ref[...])
    a = jnp.exp(log_a)
    b = jnp.sqrt(-jnp.tanh(log_a) * (a * a + 1.0)) * (gate_i * xc)

    sub = lax.broadcasted_iota(jnp.int32, (tb, LRU_WIDTH), 0) % SUBLANES
    shift = 1
    while shift < SUBLANES:
        a_prev = pltpu.roll(a, shift, axis=0)
        b_prev = pltpu.roll(b, shift, axis=0)
        take = sub >= shift
        b = jnp.where(take, a * b_prev + b, b)
        a = jnp.where(take, a * a_prev, a)
        shift *= 2
    a_ref[...] = a.reshape(tb // SUBLANES, SUBLANES, LRU_WIDTH)
    b_ref[...] = b.reshape(tb // SUBLANES, SUBLANES, LRU_WIDTH)

    def chain(g, h_prev):
        h = a_ref[g] * h_prev + b_ref[g]
        b_ref[g] = h
        return jnp.broadcast_to(h[SUBLANES - 1:SUBLANES, :], (SUBLANES, LRU_WIDTH))

    carry_ref[...] = lax.fori_loop(0, tb // SUBLANES, chain, carry_ref[...])
    hseq = b_ref[...].reshape(tb, LRU_WIDTH)

    y = y_ref[...]
    gelu = y * (0.5 * (1.0 + jnp.tanh(0.7978845608028654 * (y + 0.044715 * (y * y * y)))))
    o_ref[...] = (gelu * hseq).astype(BF16)


def _lru(proj, w, batch, seq):
    t = batch * seq
    tb = ROW_BLOCK
    nblk = seq // tb
    cols = lambda offset: pl.BlockSpec((tb, LRU_WIDTH), lambda b, i: (b * nblk + i, offset // LRU_WIDTH))
    full = lambda arr: pl.BlockSpec(arr.shape, lambda b, i: (0,) * arr.ndim)
    consts = [w["conv_w"], w["conv_b"], w["w_a"], w["b_a"], w["w_i"], w["b_i"], w["lam"]]
    return pl.pallas_call(
        _lru_kernel,
        grid=(batch, nblk),
        in_specs=[cols(OD_X), cols(OD_Y)] + [full(a) for a in consts],
        out_specs=pl.BlockSpec((tb, LRU_WIDTH), lambda b, i: (b * nblk + i, 0)),
        out_shape=jax.ShapeDtypeStruct((t, LRU_WIDTH), BF16),
        scratch_shapes=[pltpu.VMEM((tb + HALO, LRU_WIDTH), F32),
                        pltpu.VMEM((SUBLANES, LRU_WIDTH), F32),
                        pltpu.VMEM((tb // SUBLANES, SUBLANES, LRU_WIDTH), F32),
                        pltpu.VMEM((tb // SUBLANES, SUBLANES, LRU_WIDTH), F32)],
        compiler_params=_params("parallel", "arbitrary"),
        name="lru",
    )(proj, proj, *consts)


def _post_kernel(h_ref, a_ref, b_ref, wo_ref, g_ref, w1_ref, w2_ref, gf_ref, o_ref, *, final_norm):
    half = a_ref.shape[1]
    sub = h_ref.shape[0] // POST_ROW_SPLIT
    for r in range(POST_ROW_SPLIT):
        rows = slice(r * sub, (r + 1) * sub)
        mix = _dot(a_ref[rows, :], wo_ref[0:half, :]) + _dot(b_ref[rows, :], wo_ref[half:2 * half, :])
        h1 = h_ref[rows, :] + mix
        hn = (_rms(h1) * g_ref[...]).astype(BF16)
        out = h1
        for c in range(D_FF // FF_BLOCK):
            cols = slice(c * FF_BLOCK, (c + 1) * FF_BLOCK)
            hidden = _dot(hn, w1_ref[:, cols])
            act = jnp.square(jnp.maximum(hidden, 0.0)).astype(BF16)
            out = out + _dot(act, w2_ref[cols, :])
        if final_norm:
            out = _rms(out) * gf_ref[...]
        o_ref[rows, :] = out


def _post(h, a, b, w_out, norm_g, w1, w2, final_g, final_norm):
    t = h.shape[0]
    tm = ROW_BLOCK
    rows = lambda width: pl.BlockSpec((tm, width), lambda i: (i, 0))
    vec = pl.BlockSpec((1, D_MODEL), lambda i: (0, 0))
    resident = lambda arr: pl.BlockSpec(arr.shape, lambda i: (0, 0), pipeline_mode=pl.Buffered(1))
    return pl.pallas_call(
        functools.partial(_post_kernel, final_norm=final_norm),
        grid=(t // tm,),
        in_specs=[rows(D_MODEL), rows(a.shape[1]), rows(b.shape[1]), resident(w_out), vec, resident(w1),
                  resident(w2), vec],
        out_specs=rows(D_MODEL),
        out_shape=jax.ShapeDtypeStruct((t, D_MODEL), F32),
        compiler_params=_params("parallel"),
        name="post_mlp",
    )(h, a, b, w_out, norm_g, w1, w2, final_g)


def _even_weights(w_in, conv_w, q_norm_g, w_uq, kv_norm_g, w_ukv):
    half = MLA_ROPE // 2
    z = lambda *shape: jnp.zeros(shape, F32)
    kr = w_in[:, EV_KR:EV_KR + MLA_ROPE]
    kr1, kr2 = kr[:, :half], kr[:, half:]
    tail = HEAD_PAD - MLA_NOPE - MLA_ROPE
    kr_blk = jnp.concatenate([z(D_MODEL, MLA_NOPE), kr1, kr2, z(D_MODEL, tail)], axis=1)
    krs_blk = jnp.concatenate([z(D_MODEL, MLA_NOPE), kr2, kr1, z(D_MODEL, tail)], axis=1)
    w_in_pad = jnp.concatenate([w_in[:, :EV_KR], kr_blk, krs_blk], axis=1)

    wq = w_uq.reshape(MLA_Q_RANK, MLA_HEADS, MLA_NOPE + MLA_ROPE)
    nope, r1, r2 = wq[..., :MLA_NOPE], wq[..., MLA_NOPE:MLA_NOPE + half], wq[..., MLA_NOPE + half:]
    zq = z(MLA_Q_RANK, MLA_HEADS, tail)
    w_q = jnp.concatenate([nope, r1, r2, zq], axis=-1).reshape(MLA_Q_RANK, MLA_PAD)
    w_q_swap = jnp.concatenate([jnp.zeros_like(nope), r2, r1, zq], axis=-1).reshape(MLA_Q_RANK, MLA_PAD)

    wkv = w_ukv.reshape(MLA_KV_RANK, MLA_HEADS, MLA_NOPE + MLA_V)
    k_nope, v = wkv[..., :MLA_NOPE], wkv[..., MLA_NOPE:]
    w_k = jnp.concatenate([k_nope, z(MLA_KV_RANK, MLA_HEADS, HEAD_PAD - MLA_NOPE)], axis=-1)
    zv = jnp.zeros_like(v)
    odd_head = (jnp.arange(MLA_HEADS) % 2 == 1)[None, :, None]
    w_v = jnp.concatenate([jnp.where(odd_head, zv, v), jnp.where(odd_head, v, zv)], axis=-1)
    return {
        "w_in": w_in_pad.astype(BF16), "conv_w": conv_w, "q_norm_g": q_norm_g[None, :],
        "kv_norm_g": kv_norm_g[None, :], "w_q": w_q.astype(BF16), "w_q_swap": w_q_swap.astype(BF16),
        "w_k": w_k.reshape(MLA_KV_RANK, MLA_PAD).astype(BF16), "w_v": w_v.reshape(MLA_KV_RANK, MLA_PAD).astype(BF16),
    }


def _block_diag(w):
    n, r, c = w.shape
    eye = jnp.eye(n, dtype=w.dtype)
    return (eye[:, None, :, None] * w[:, :, None, :]).reshape(n * r, n * c)


def _odd_weights(w_in, gate_up, gate_bias, gla_norm_g, conv_w, conv_b, w_a, b_a, w_i, b_i, lam):
    dk, dv = GLA_HEADS * GLA_DK, GLA_HEADS * GLA_DV
    o = 0
    parts = {}
    for name, width in (("q", dk), ("k", dk), ("v", dv), ("g", GLA_GATE_RANK), ("og", dv), ("x", LRU_WIDTH),
                        ("y", LRU_WIDTH)):
        parts[name] = w_in[:, o:o + width]
        o += width
    g_pad = jnp.zeros((D_MODEL, LANES - GLA_GATE_RANK), F32)
    w_in_pad = jnp.concatenate([parts["v"], parts["og"], parts["x"], parts["y"], parts["q"], parts["k"], parts["g"],
                                g_pad], axis=1)
    gate_up_pad = jnp.concatenate([gate_up, jnp.zeros((LANES - GLA_GATE_RANK, dk), F32)], axis=0)
    return {
        "w_in": w_in_pad.astype(BF16), "gate_up": gate_up_pad.astype(BF16), "gate_bias": gate_bias[None, :],
        "gla_norm_g": gla_norm_g[None, :], "conv_w": conv_w, "conv_b": conv_b[None, :],
        "w_a": _block_diag(w_a).astype(BF16), "b_a": b_a[None, :], "w_i": _block_diag(w_i).astype(BF16),
        "b_i": b_i[None, :], "lam": lam[None, :],
    }


def kernel(x, positions, mixer_norm_g, mlp_norm_g, mlp_w1, mlp_w2, final_norm_g, ev_w_in, ev_conv_w, mla_q_norm_g, mla_w_uq, mla_kv_norm_g, mla_w_ukv, ev_w_out, od_w_in, gla_w_gate_up, gla_b_gate, gla_norm_g, lru_conv_w, lru_conv_b, lru_w_a, lru_b_a, lru_w_i, lru_b_i, lru_lambda, od_w_out):
    batch, seq, d_model = x.shape
    depth = mixer_norm_g.shape[0]
    assert d_model == D_MODEL and seq % ROW_BLOCK == 0 and seq % ATTN_BLOCK == 0 and seq % GLA_BLOCK == 0
    h = x.reshape(batch * seq, D_MODEL)
    cos, sin = _rope_tables(positions)
    final_g = final_norm_g[None, :]
    for layer in range(depth):
        j = layer // 2
        norm_g = mixer_norm_g[layer][None, :]
        if layer % 2 == 0:
            w = _even_weights(ev_w_in[j], ev_conv_w[j], mla_q_norm_g[j], mla_w_uq[j], mla_kv_norm_g[j], mla_w_ukv[j])
            mix_a, q, k, v = _even_pre(h, norm_g, w, cos, sin, batch, seq)
            mix_b = _attention(q, k, v, batch, seq)
            w_out = ev_w_out[j]
        else:
            w = _odd_weights(od_w_in[j], gla_w_gate_up[j], gla_b_gate[j], gla_norm_g[j], lru_conv_w[j],
                             lru_conv_b[j], lru_w_a[j], lru_b_a[j], lru_w_i[j], lru_b_i[j], lru_lambda[j])
            proj = _norm_matmul(h, norm_g, w["w_in"])
            mix_a = _gla(proj, w, batch, seq)
            mix_b = _lru(proj, w, batch, seq)
            w_out = od_w_out[j]
        h = _post(h, mix_a, mix_b, w_out.astype(BF16), mlp_norm_g[layer][None, :], mlp_w1[layer].astype(BF16),
                  mlp_w2[layer].astype(BF16), final_g, final_norm=(layer == depth - 1))
    return h.reshape(batch, seq, D_MODEL)
```

```python
import functools

import jax
import jax.numpy as jnp
from jax import lax
from jax.experimental import pallas as pl
from jax.experimental.pallas import tpu as pltpu

F32 = jnp.float32
BF16 = jnp.bfloat16

D_MODEL = 1024
D_FF = 4 * D_MODEL
NORM_EPS = 1e-6

SC_WIDTH = 512
SC_KERNEL = 3
MLA_HEADS = 8
MLA_NOPE = 64
MLA_ROPE = 32
MLA_V = 64
MLA_Q_RANK = 384
MLA_KV_RANK = 256
ROPE_THETA = 10000.0
QK_SCALE_LOG2E = (MLA_NOPE + MLA_ROPE) ** -0.5 * 1.4426950408889634

GLA_HEADS = 4
GLA_DK = 64
GLA_DV = 128
GLA_GATE_RANK = 16
GLA_TAU = 16.0
GLA_CHUNK = 64
LRU_WIDTH = 512
LRU_BLOCKS = 8
LRU_KERNEL = 4
LRU_C = 8.0

LANES = 128
SUBLANES = 8
VMEM_LIMIT_BYTES = 56 * 1024 * 1024

HEAD_PAD = LANES
MLA_PAD = MLA_HEADS * HEAD_PAD

EV_GB, EV_GC, EV_U = 0, 512, 1024
EV_CQ = 1536
EV_CKV = EV_CQ + MLA_Q_RANK
EV_KR = EV_CKV + MLA_KV_RANK
EV_KRS = EV_KR + LANES
EV_IN_PAD = EV_KRS + LANES

OD_V, OD_OG, OD_X, OD_Y = 0, 512, 1024, 1536
OD_Q, OD_K, OD_G = 2048, 2304, 2560
OD_IN_PAD = OD_G + LANES

ROW_BLOCK = 512
FF_BLOCK = 1024
POST_ROW_SPLIT = 2
ATTN_BLOCK = 512
ATTN_HEAD_GROUP = 8
ATTN_LOOKAHEAD = 2
GLA_BLOCK = 256


def _rms(x):
    return x * lax.rsqrt(jnp.mean(x * x, axis=-1, keepdims=True) + NORM_EPS)


def _dot(a, b):
    return jnp.dot(a, b, preferred_element_type=F32)


def _dot_nt(a, b):
    return lax.dot_general(a, b, (((1,), (1,)), ((), ())), preferred_element_type=F32)


def _dot_tn(a, b):
    return lax.dot_general(a, b, (((0,), (0,)), ((), ())), preferred_element_type=F32)


def _sigmoid(x):
    return 0.5 * jnp.tanh(0.5 * x) + 0.5


def _softplus(x):
    return jnp.maximum(x, 0.0) + jnp.log1p(jnp.exp(-jnp.abs(x)))


def _params(*semantics, flags=None):
    return pltpu.CompilerParams(dimension_semantics=semantics, vmem_limit_bytes=VMEM_LIMIT_BYTES, flags=flags)


def _causal_conv(tail_ref, x, w_ref, first_block):
    rows, width = x.shape
    taps = w_ref.shape[0]
    groups = rows // SUBLANES

    @pl.when(first_block)
    def _():
        tail_ref[...] = jnp.zeros((SUBLANES, width), F32)

    x3 = x.reshape(groups, SUBLANES, width)
    with_tail = jnp.concatenate([tail_ref[...][None], x3], axis=0)
    tail_ref[...] = x3[groups - 1]
    sub = lax.broadcasted_iota(jnp.int32, (groups, SUBLANES, width), 1)
    out = w_ref[taps - 1:taps, :] * x3
    for k in range(taps - 1):
        shift = taps - 1 - k
        rolled = pltpu.roll(with_tail, shift, axis=1)
        shifted = jnp.where(sub >= shift, rolled[1:], rolled[:-1])
        out = out + w_ref[k:k + 1, :] * shifted
    return out.reshape(rows, width)


def _rope_table_kernel(pos_ref, freq_ref, sign_ref, cos_ref, sin_ref):
    ang = pos_ref[...].astype(F32) * freq_ref[...]
    cos_ref[...] = jnp.cos(ang)
    sin_ref[...] = jnp.sin(ang) * sign_ref[...]


def _rope_tables(positions):
    b, s = positions.shape
    t = b * s
    inv_freq = ROPE_THETA ** (-jnp.arange(0, MLA_ROPE, 2, dtype=F32) / MLA_ROPE)
    half = MLA_ROPE // 2
    zeros_lo = jnp.zeros((MLA_NOPE,), F32)
    zeros_hi = jnp.zeros((HEAD_PAD - MLA_NOPE - MLA_ROPE,), F32)
    freq = jnp.concatenate([zeros_lo, inv_freq, inv_freq, zeros_hi])[None, :]
    sign = jnp.concatenate([zeros_lo, -jnp.ones((half,), F32), jnp.ones((half,), F32), zeros_hi])[None, :]
    ts = ROW_BLOCK
    row = pl.BlockSpec((ts, 1), lambda i: (i, 0))
    lane = pl.BlockSpec((1, HEAD_PAD), lambda i: (0, 0))
    out = pl.BlockSpec((ts, HEAD_PAD), lambda i: (i, 0))
    return pl.pallas_call(
        _rope_table_kernel,
        grid=(t // ts,),
        in_specs=[row, lane, lane],
        out_specs=[out, out],
        out_shape=[jax.ShapeDtypeStruct((t, HEAD_PAD), F32)] * 2,
        compiler_params=_params("parallel"),
        name="rope_tables",
    )(positions.reshape(t, 1), freq, sign)


def _even_pre_kernel(h_ref, g_ref, win_ref, cw_ref, qg_ref, kvg_ref, wq_ref, wqs_ref, wk_ref, wv_ref,
                     cos_ref, sin_ref, sc_ref, q_ref, k_ref, v_ref, ext_ref):
    hn = (_rms(h_ref[...]) * g_ref[...]).astype(BF16)
    proj = _dot(hn, win_ref[...])

    gate_b = proj[:, EV_GB:EV_GB + SC_WIDTH]
    cu = proj[:, EV_GC:EV_GC + SC_WIDTH] * proj[:, EV_U:EV_U + SC_WIDTH]
    conv = _causal_conv(ext_ref, cu, cw_ref, pl.program_id(1) == 0)
    sc_ref[...] = (gate_b * conv).astype(BF16)

    cos = cos_ref[...]
    sin = sin_ref[...]
    cqn = (_rms(proj[:, EV_CQ:EV_CQ + MLA_Q_RANK]) * qg_ref[...]).astype(BF16)
    q0 = _dot(cqn, wq_ref[...])
    q1 = _dot(cqn, wqs_ref[...])
    ckvn = (_rms(proj[:, EV_CKV:EV_CKV + MLA_KV_RANK]) * kvg_ref[...]).astype(BF16)
    kn = _dot(ckvn, wk_ref[...])
    vt = _dot_nt(wv_ref[...], ckvn)
    feat = lax.broadcasted_iota(jnp.int32, vt.shape, 0)
    ones_feat = (1 - (feat // HEAD_PAD) % 2) * MLA_V
    v_ref[...] = jnp.where(feat % HEAD_PAD == ones_feat, 1.0, vt).astype(BF16)
    k_rope = proj[:, EV_KR:EV_KR + LANES] * cos + proj[:, EV_KRS:EV_KRS + LANES] * sin
    for head in range(MLA_HEADS):
        sl = slice(head * HEAD_PAD, (head + 1) * HEAD_PAD)
        q_ref[:, sl] = ((q0[:, sl] * cos + q1[:, sl] * sin) * QK_SCALE_LOG2E).astype(BF16)
        k_ref[:, sl] = (kn[:, sl] + k_rope).astype(BF16)


def _even_pre(h, norm_g, w, cos, sin, batch, seq):
    t = batch * seq
    tm = ROW_BLOCK
    nblk = seq // tm
    rows = lambda width: pl.BlockSpec((tm, width), lambda b, i: (b * nblk + i, 0))
    full = lambda arr: pl.BlockSpec(arr.shape, lambda b, i: (0,) * arr.ndim)
    consts = [norm_g, w["w_in"], w["conv_w"], w["q_norm_g"], w["kv_norm_g"], w["w_q"], w["w_q_swap"], w["w_k"],
              w["w_v"]]
    return pl.pallas_call(
        _even_pre_kernel,
        grid=(batch, nblk),
        in_specs=[rows(D_MODEL)] + [full(a) for a in consts] + [rows(HEAD_PAD), rows(HEAD_PAD)],
        out_specs=[rows(SC_WIDTH), rows(MLA_PAD), rows(MLA_PAD),
                   pl.BlockSpec((None, None, MLA_PAD, tm), lambda b, i: (b, i, 0, 0))],
        out_shape=[jax.ShapeDtypeStruct((t, SC_WIDTH), BF16)] + [jax.ShapeDtypeStruct((t, MLA_PAD), BF16)] * 2
        + [jax.ShapeDtypeStruct((batch, nblk, MLA_PAD, tm), BF16)],
        scratch_shapes=[pltpu.VMEM((SUBLANES, SC_WIDTH), F32)],
        compiler_params=_params("parallel", "arbitrary"),
        name="even_pre",
    )(h, *consts, cos, sin)


def _attn_kernel(q_ref, k_ref, v_ref, o_ref, acc_ref):
    tq = q_ref.shape[0]
    qi = pl.program_id(1)
    key = lax.broadcasted_iota(jnp.int32, (tq, tq), 0)
    query = lax.broadcasted_iota(jnp.int32, (tq, tq), 1)
    causal = key <= query
    feat = lax.broadcasted_iota(jnp.int32, (HEAD_PAD, tq), 0)

    def scores(head, j, masked):
        sl = slice(head * HEAD_PAD, (head + 1) * HEAD_PAD)
        start = pl.multiple_of(j * tq, tq)
        s = _dot_nt(k_ref[pl.ds(start, tq), sl], q_ref[:, sl])
        return jnp.where(causal, s, -jnp.inf) if masked else s

    def accumulate(head, j, m, s):
        sl = slice(head * HEAD_PAD, (head + 1) * HEAD_PAD)
        m_new = jnp.maximum(m, jnp.max(s, axis=0, keepdims=True))
        p = jnp.exp2(s - m_new)
        alpha = jnp.exp2(m - m_new)
        acc_ref[head] = alpha * acc_ref[head] + _dot(v_ref[j, sl, :], p.astype(BF16))
        return m_new

    def tile(heads, j, ms, masked):
        out = []
        pending = [scores(h, j, masked) for h in heads[:ATTN_LOOKAHEAD]]
        for idx, (h, m) in enumerate(zip(heads, ms)):
            if idx + ATTN_LOOKAHEAD < len(heads):
                pending.append(scores(heads[idx + ATTN_LOOKAHEAD], j, masked))
            out.append(accumulate(h, j, m, pending.pop(0)))
        return tuple(out)

    for group in range(MLA_HEADS // ATTN_HEAD_GROUP):
        heads = tuple(range(group * ATTN_HEAD_GROUP, (group + 1) * ATTN_HEAD_GROUP))
        for h in heads:
            acc_ref[h] = jnp.zeros((HEAD_PAD, tq), F32)
        m0 = jnp.full((1, tq), -jnp.inf, F32)
        ms = lax.fori_loop(0, qi, lambda j, ms: tile(heads, j, ms, False), (m0,) * ATTN_HEAD_GROUP)
        tile(heads, qi, ms, True)
        for even in heads[::2]:
            acc_even = acc_ref[even]
            acc_odd = acc_ref[even + 1]
            out = jnp.where(feat < MLA_V, acc_even / acc_even[MLA_V:MLA_V + 1, :], acc_odd / acc_odd[0:1, :])
            o_ref[:, (even // 2) * LANES:(even // 2 + 1) * LANES] = out.T.astype(BF16)


def _attention(q, k, vt, batch, seq):
    tq = ATTN_BLOCK
    q3, k3 = (a.reshape(batch, seq, MLA_PAD) for a in (q, k))
    out = pl.pallas_call(
        _attn_kernel,
        grid=(batch, seq // tq),
        in_specs=[pl.BlockSpec((None, tq, MLA_PAD), lambda b, i: (b, i, 0)),
                  pl.BlockSpec((None, seq, MLA_PAD), lambda b, i: (b, 0, 0)),
                  pl.BlockSpec((None,) + vt.shape[1:], lambda b, i: (b, 0, 0, 0))],
        out_specs=pl.BlockSpec((None, tq, MLA_HEADS * MLA_V), lambda b, i: (b, i, 0)),
        out_shape=jax.ShapeDtypeStruct((batch, seq, MLA_HEADS * MLA_V), BF16),
        scratch_shapes=[pltpu.VMEM((MLA_HEADS, HEAD_PAD, tq), F32)],
        compiler_params=_params("parallel", "arbitrary"),
        name="mla_attention",
    )(q3, k3, vt)
    return out.reshape(batch * seq, MLA_HEADS * MLA_V)


def _norm_matmul_kernel(h_ref, g_ref, w_ref, o_ref):
    hn = (_rms(h_ref[...]) * g_ref[...]).astype(BF16)
    o_ref[...] = _dot(hn, w_ref[...])


def _norm_matmul(h, norm_g, w):
    t = h.shape[0]
    tm = ROW_BLOCK
    n = w.shape[1]
    return pl.pallas_call(
        _norm_matmul_kernel,
        grid=(t // tm,),
        in_specs=[pl.BlockSpec((tm, D_MODEL), lambda i: (i, 0)),
                  pl.BlockSpec((1, D_MODEL), lambda i: (0, 0)),
                  pl.BlockSpec(w.shape, lambda i: (0, 0))],
        out_specs=pl.BlockSpec((tm, n), lambda i: (i, 0)),
        out_shape=jax.ShapeDtypeStruct((t, n), F32),
        compiler_params=_params("parallel"),
        name="norm_matmul",
    )(h, norm_g, w)


def _gla_kernel(q_ref, k_ref, v_ref, g_ref, og_ref, gup_ref, gbias_ref, ng_ref, o_ref, state_ref):
    tb = q_ref.shape[0]
    n_pairs = GLA_HEADS // 2
    pair_dk = 2 * GLA_DK
    pair_dv = 2 * GLA_DV

    @pl.when(pl.program_id(1) == 0)
    def _():
        state_ref[...] = jnp.zeros(state_ref.shape, F32)

    gate = _dot(g_ref[...].astype(BF16), gup_ref[...]) + gbias_ref[...]
    log_a = (jnp.minimum(gate, 0.0) - jnp.log1p(jnp.exp(-jnp.abs(gate)))) / GLA_TAU

    n_chunks = tb // GLA_CHUNK
    row = lax.broadcasted_iota(jnp.int32, (tb, tb), 0)
    col = lax.broadcasted_iota(jnp.int32, (tb, tb), 1)
    tril = ((row // GLA_CHUNK) == (col // GLA_CHUNK)) & (col <= row)
    tril_bf = tril.astype(BF16)
    cum = None
    rest = log_a
    for _ in range(3):
        piece = rest.astype(BF16)
        rest = rest - piece.astype(F32)
        part = _dot(tril_bf, piece)
        cum = part if cum is None else cum + part
    chunk_last = [cum[(c + 1) * GLA_CHUNK - 1:(c + 1) * GLA_CHUNK, :] for c in range(n_chunks)]
    last = jnp.concatenate([jnp.broadcast_to(t, (GLA_CHUNK, t.shape[1])) for t in chunk_last], axis=0)

    q = q_ref[...] * (GLA_DK ** -0.5)
    k = k_ref[...]
    q_in = q * jnp.exp(cum)
    k_in = k * jnp.exp(-cum)
    k_out = k * jnp.exp(last - cum)
    v = v_ref[...].astype(BF16)

    lane = lax.broadcasted_iota(jnp.int32, (tb, pair_dk), 1)
    state_rows = lax.broadcasted_iota(jnp.int32, (pair_dv, pair_dk), 0)
    state_lanes = lax.broadcasted_iota(jnp.int32, (pair_dv, pair_dk), 1)
    own_block = (state_rows < GLA_DV) == (state_lanes < GLA_DK)

    ksl = [slice(p * pair_dk, (p + 1) * pair_dk) for p in range(n_pairs)]
    vsl = [slice(p * pair_dv, (p + 1) * pair_dv) for p in range(n_pairs)]

    states = [state_ref[p] for p in range(n_pairs)]
    inter = [[] for _ in range(n_pairs)]
    for c in range(n_chunks):
        rows = slice(c * GLA_CHUNK, (c + 1) * GLA_CHUNK)
        for p in range(n_pairs):
            inter[p].append(_dot_nt(q_in[rows, ksl[p]].astype(BF16), states[p].astype(BF16)))
            kv = jnp.where(own_block, _dot_tn(v[rows, vsl[p]], k_out[rows, ksl[p]].astype(BF16)), 0.0)
            states[p] = states[p] * jnp.exp(chunk_last[c][:, ksl[p]]) + kv
    for p in range(n_pairs):
        state_ref[p] = states[p]

    for p in range(n_pairs):
        qp, kp = q_in[:, ksl[p]], k_in[:, ksl[p]]
        o_inter = jnp.concatenate(inter[p], axis=0)
        for sub in range(2):
            mine = (lane < GLA_DK) if sub == 0 else (lane >= GLA_DK)
            qh = jnp.where(mine, qp, 0.0).astype(BF16)
            kh = jnp.where(mine, kp, 0.0).astype(BF16)
            att = jnp.where(tril, _dot_nt(qh, kh), 0.0)
            head = 2 * p + sub
            hsl = slice(head * GLA_DV, (head + 1) * GLA_DV)
            o = _dot(att.astype(BF16), v[:, hsl]) + o_inter[:, sub * GLA_DV:(sub + 1) * GLA_DV]
            o = _rms(o) * ng_ref[...]
            og = og_ref[:, hsl]
            o_ref[:, hsl] = (o * (og * _sigmoid(og))).astype(BF16)


def _gla(proj, w, batch, seq):
    t = batch * seq
    tb = GLA_BLOCK
    nblk = seq // tb
    cols = lambda width, offset: pl.BlockSpec((tb, width), lambda b, i: (b * nblk + i, offset // width))
    full = lambda arr: pl.BlockSpec(arr.shape, lambda b, i: (0,) * arr.ndim)
    consts = [w["gate_up"], w["gate_bias"], w["gla_norm_g"]]
    width_v = GLA_HEADS * GLA_DV
    width_k = GLA_HEADS * GLA_DK
    return pl.pallas_call(
        _gla_kernel,
        grid=(batch, nblk),
        in_specs=[cols(width_k, OD_Q), cols(width_k, OD_K), cols(width_v, OD_V), cols(LANES, OD_G),
                  cols(width_v, OD_OG)] + [full(a) for a in consts],
        out_specs=pl.BlockSpec((tb, width_v), lambda b, i: (b * nblk + i, 0)),
        out_shape=jax.ShapeDtypeStruct((t, width_v), BF16),
        scratch_shapes=[pltpu.VMEM((GLA_HEADS // 2, 2 * GLA_DV, 2 * GLA_DK), F32)],
        compiler_params=_params("parallel", "arbitrary"),
        name="gla",
    )(proj, proj, proj, proj, proj, *consts)


def _lru_kernel(x_ref, y_ref, cw_ref, cb_ref, wa_ref, ba_ref, wi_ref, bi_ref, lam_ref, o_ref,
                ext_ref, carry_ref, a_ref, b_ref):
    tb = x_ref.shape[0]
    first = pl.program_id(1) == 0

    @pl.when(first)
    def _():
        carry_ref[...] = jnp.zeros(carry_ref.shape, F32)

    xc = _causal_conv(ext_ref, x_ref[...], cw_ref, first) + cb_ref[...]
    xcb = xc.astype(BF16)
    r = _sigmoid(_dot(xcb, wa_ref[...]) + ba_ref[...])
    gate_i = _sigmoid(_dot(xcb, wi_ref[...]) + bi_ref[...])
    log_a = (-LRU_C * r) * _softplus(-lam_ref[...])
    a = jnp.exp(log_a)
    b = jnp.sqrt(-jnp.tanh(log_a) * (a * a + 1.0)) * (gate_i * xc)

    groups = tb // SUBLANES
    a = a.reshape(groups, SUBLANES, LRU_WIDTH)
    b = b.reshape(groups, SUBLANES, LRU_WIDTH)
    sub = lax.broadcasted_iota(jnp.int32, (groups, SUBLANES, LRU_WIDTH), 1)
    shift = 1
    while shift < SUBLANES:
        a_prev = pltpu.roll(a, shift, axis=1)
        b_prev = pltpu.roll(b, shift, axis=1)
        take = sub >= shift
        b = jnp.where(take, a * b_prev + b, b)
        a = jnp.where(take, a * a_prev, a)
        shift *= 2
    a_ref[...] = a
    b_ref[...] = b

    def chain(g, h_prev):
        h = a_ref[g] * h_prev + b_ref[g]
        b_ref[g] = h
        return jnp.broadcast_to(h[SUBLANES - 1:SUBLANES, :], (SUBLANES, LRU_WIDTH))

    carry_ref[...] = lax.fori_loop(0, tb // SUBLANES, chain, carry_ref[...])
    hseq = b_ref[...].reshape(tb, LRU_WIDTH)

    y = y_ref[...]
    gelu = y * (0.5 * (1.0 + jnp.tanh(0.7978845608028654 * (y + 0.044715 * (y * y * y)))))
    o_ref[...] = (gelu * hseq).astype(BF16)


def _lru(proj, w, batch, seq):
    t = batch * seq
    tb = ROW_BLOCK
    nblk = seq // tb
    cols = lambda offset: pl.BlockSpec((tb, LRU_WIDTH), lambda b, i: (b * nblk + i, offset // LRU_WIDTH))
    full = lambda arr: pl.BlockSpec(arr.shape, lambda b, i: (0,) * arr.ndim)
    consts = [w["conv_w"], w["conv_b"], w["w_a"], w["b_a"], w["w_i"], w["b_i"], w["lam"]]
    return pl.pallas_call(
        _lru_kernel,
        grid=(batch, nblk),
        in_specs=[cols(OD_X), cols(OD_Y)] + [full(a) for a in consts],
        out_specs=pl.BlockSpec((tb, LRU_WIDTH), lambda b, i: (b * nblk + i, 0)),
        out_shape=jax.ShapeDtypeStruct((t, LRU_WIDTH), BF16),
        scratch_shapes=[pltpu.VMEM((SUBLANES, LRU_WIDTH), F32),
                        pltpu.VMEM((SUBLANES, LRU_WIDTH), F32),
                        pltpu.VMEM((tb // SUBLANES, SUBLANES, LRU_WIDTH), F32),
                        pltpu.VMEM((tb // SUBLANES, SUBLANES, LRU_WIDTH), F32)],
        compiler_params=_params("parallel", "arbitrary"),
        name="lru",
    )(proj, proj, *consts)


def _post_kernel(h_ref, a_ref, b_ref, wo_ref, g_ref, w1_ref, w2_ref, gf_ref, o_ref, *, final_norm):
    half = a_ref.shape[1]
    sub = h_ref.shape[0] // POST_ROW_SPLIT
    for r in range(POST_ROW_SPLIT):
        rows = slice(r * sub, (r + 1) * sub)
        mix = _dot(a_ref[rows, :], wo_ref[0:half, :]) + _dot(b_ref[rows, :], wo_ref[half:2 * half, :])
        h1 = h_ref[rows, :] + mix
        hn = (_rms(h1) * g_ref[...]).astype(BF16)
        out = h1
        for c in range(D_FF // FF_BLOCK):
            cols = slice(c * FF_BLOCK, (c + 1) * FF_BLOCK)
            hidden = _dot(hn, w1_ref[:, cols])
            act = jnp.square(jnp.maximum(hidden, 0.0)).astype(BF16)
            out = out + _dot(act, w2_ref[cols, :])
        if final_norm:
            out = _rms(out) * gf_ref[...]
        o_ref[rows, :] = out


def _post(h, a, b, w_out, norm_g, w1, w2, final_g, final_norm):
    t = h.shape[0]
    tm = ROW_BLOCK
    rows = lambda width: pl.BlockSpec((tm, width), lambda i: (i, 0))
    vec = pl.BlockSpec((1, D_MODEL), lambda i: (0, 0))
    resident = lambda arr: pl.BlockSpec(arr.shape, lambda i: (0, 0), pipeline_mode=pl.Buffered(1))
    return pl.pallas_call(
        functools.partial(_post_kernel, final_norm=final_norm),
        grid=(t // tm,),
        in_specs=[rows(D_MODEL), rows(a.shape[1]), rows(b.shape[1]), resident(w_out), vec, resident(w1),
                  resident(w2), vec],
        out_specs=rows(D_MODEL),
        out_shape=jax.ShapeDtypeStruct((t, D_MODEL), F32),
        compiler_params=_params("parallel"),
        name="post_mlp",
    )(h, a, b, w_out, norm_g, w1, w2, final_g)


def _even_weights(w_in, conv_w, q_norm_g, w_uq, kv_norm_g, w_ukv):
    half = MLA_ROPE // 2
    z = lambda *shape: jnp.zeros(shape, F32)
    kr = w_in[:, EV_KR:EV_KR + MLA_ROPE]
    kr1, kr2 = kr[:, :half], kr[:, half:]
    tail = HEAD_PAD - MLA_NOPE - MLA_ROPE
    kr_blk = jnp.concatenate([z(D_MODEL, MLA_NOPE), kr1, kr2, z(D_MODEL, tail)], axis=1)
    krs_blk = jnp.concatenate([z(D_MODEL, MLA_NOPE), kr2, kr1, z(D_MODEL, tail)], axis=1)
    w_in_pad = jnp.concatenate([w_in[:, :EV_KR], kr_blk, krs_blk], axis=1)

    wq = w_uq.reshape(MLA_Q_RANK, MLA_HEADS, MLA_NOPE + MLA_ROPE)
    nope, r1, r2 = wq[..., :MLA_NOPE], wq[..., MLA_NOPE:MLA_NOPE + half], wq[..., MLA_NOPE + half:]
    zq = z(MLA_Q_RANK, MLA_HEADS, tail)
    w_q = jnp.concatenate([nope, r1, r2, zq], axis=-1).reshape(MLA_Q_RANK, MLA_PAD)
    w_q_swap = jnp.concatenate([jnp.zeros_like(nope), r2, r1, zq], axis=-1).reshape(MLA_Q_RANK, MLA_PAD)

    wkv = w_ukv.reshape(MLA_KV_RANK, MLA_HEADS, MLA_NOPE + MLA_V)
    k_nope, v = wkv[..., :MLA_NOPE], wkv[..., MLA_NOPE:]
    w_k = jnp.concatenate([k_nope, z(MLA_KV_RANK, MLA_HEADS, HEAD_PAD - MLA_NOPE)], axis=-1)
    zv = jnp.zeros_like(v)
    odd_head = (jnp.arange(MLA_HEADS) % 2 == 1)[None, :, None]
    w_v = jnp.concatenate([jnp.where(odd_head, zv, v), jnp.where(odd_head, v, zv)], axis=-1)
    return {
        "w_in": w_in_pad.astype(BF16), "conv_w": conv_w, "q_norm_g": q_norm_g[None, :],
        "kv_norm_g": kv_norm_g[None, :], "w_q": w_q.astype(BF16), "w_q_swap": w_q_swap.astype(BF16),
        "w_k": w_k.reshape(MLA_KV_RANK, MLA_PAD).astype(BF16),
        "w_v": w_v.reshape(MLA_KV_RANK, MLA_PAD).T.astype(BF16),
    }


def _block_diag(w):
    n, r, c = w.shape
    eye = jnp.eye(n, dtype=w.dtype)
    return (eye[:, None, :, None] * w[:, :, None, :]).reshape(n * r, n * c)


def _odd_weights(w_in, gate_up, gate_bias, gla_norm_g, conv_w, conv_b, w_a, b_a, w_i, b_i, lam):
    dk, dv = GLA_HEADS * GLA_DK, GLA_HEADS * GLA_DV
    o = 0
    parts = {}
    for name, width in (("q", dk), ("k", dk), ("v", dv), ("g", GLA_GATE_RANK), ("og", dv), ("x", LRU_WIDTH),
                        ("y", LRU_WIDTH)):
        parts[name] = w_in[:, o:o + width]
        o += width
    g_pad = jnp.zeros((D_MODEL, LANES - GLA_GATE_RANK), F32)
    w_in_pad = jnp.concatenate([parts["v"], parts["og"], parts["x"], parts["y"], parts["q"], parts["k"], parts["g"],
                                g_pad], axis=1)
    gate_up_pad = jnp.concatenate([gate_up, jnp.zeros((LANES - GLA_GATE_RANK, dk), F32)], axis=0)
    return {
        "w_in": w_in_pad.astype(BF16), "gate_up": gate_up_pad.astype(BF16), "gate_bias": gate_bias[None, :],
        "gla_norm_g": gla_norm_g[None, :], "conv_w": conv_w, "conv_b": conv_b[None, :],
        "w_a": _block_diag(w_a).astype(BF16), "b_a": b_a[None, :], "w_i": _block_diag(w_i).astype(BF16),
        "b_i": b_i[None, :], "lam": lam[None, :],
    }


def kernel(x, positions, mixer_norm_g, mlp_norm_g, mlp_w1, mlp_w2, final_norm_g, ev_w_in, ev_conv_w, mla_q_norm_g, mla_w_uq, mla_kv_norm_g, mla_w_ukv, ev_w_out, od_w_in, gla_w_gate_up, gla_b_gate, gla_norm_g, lru_conv_w, lru_conv_b, lru_w_a, lru_b_a, lru_w_i, lru_b_i, lru_lambda, od_w_out):
    batch, seq, d_model = x.shape
    depth = mixer_norm_g.shape[0]
    assert d_model == D_MODEL and seq % ROW_BLOCK == 0 and seq % GLA_BLOCK == 0
    assert ATTN_BLOCK == ROW_BLOCK
    h = x.reshape(batch * seq, D_MODEL)
    cos, sin = _rope_tables(positions)
    final_g = final_norm_g[None, :]
    for layer in range(depth):
        j = layer // 2
        norm_g = mixer_norm_g[layer][None, :]
        if layer % 2 == 0:
            w = _even_weights(ev_w_in[j], ev_conv_w[j], mla_q_norm_g[j], mla_w_uq[j], mla_kv_norm_g[j], mla_w_ukv[j])
            mix_a, q, k, v = _even_pre(h, norm_g, w, cos, sin, batch, seq)
            mix_b = _attention(q, k, v, batch, seq)
            w_out = ev_w_out[j]
        else:
            w = _odd_weights(od_w_in[j], gla_w_gate_up[j], gla_b_gate[j], gla_norm_g[j], lru_conv_w[j],
                             lru_conv_b[j], lru_w_a[j], lru_b_a[j], lru_w_i[j], lru_b_i[j], lru_lambda[j])
            proj = _norm_matmul(h, norm_g, w["w_in"])
            mix_a = _gla(proj, w, batch, seq)
            mix_b = _lru(proj, w, batch, seq)
            w_out = od_w_out[j]
        h = _post(h, mix_a, mix_b, w_out.astype(BF16), mlp_norm_g[layer][None, :], mlp_w1[layer].astype(BF16),
                  mlp_w2[layer].astype(BF16), final_g, final_norm=(layer == depth - 1))
    return h.reshape(batch, seq, D_MODEL)
```

```python
import functools

import jax
import jax.numpy as jnp
from jax import lax
from jax.experimental import pallas as pl
from jax.experimental.pallas import tpu as pltpu

F32 = jnp.float32
BF16 = jnp.bfloat16

D_MODEL = 1024
D_FF = 4 * D_MODEL
NORM_EPS = 1e-6

SC_WIDTH = 512
SC_KERNEL = 3
MLA_HEADS = 8
MLA_NOPE = 64
MLA_ROPE = 32
MLA_V = 64
MLA_Q_RANK = 384
MLA_KV_RANK = 256
ROPE_THETA = 10000.0
QK_SCALE_LOG2E = (MLA_NOPE + MLA_ROPE) ** -0.5 * 1.4426950408889634

GLA_HEADS = 4
GLA_DK = 64
GLA_DV = 128
GLA_GATE_RANK = 16
GLA_TAU = 16.0
GLA_CHUNK = 64
LRU_WIDTH = 512
LRU_BLOCKS = 8
LRU_KERNEL = 4
LRU_C = 8.0

LANES = 128
SUBLANES = 8
VMEM_LIMIT_BYTES = 56 * 1024 * 1024

HEAD_PAD = LANES
MLA_PAD = MLA_HEADS * HEAD_PAD

EV_GB, EV_GC, EV_U = 0, 512, 1024
EV_CQ = 1536
EV_CKV = EV_CQ + MLA_Q_RANK
EV_KR = EV_CKV + MLA_KV_RANK
EV_KRS = EV_KR + LANES
EV_IN_PAD = EV_KRS + LANES

OD_X, OD_Y, OD_V, OD_OG = 0, 512, 1024, 1536
OD_Q, OD_K, OD_G = 2048, 2304, 2560
OD_IN_PAD = OD_G + LANES

ROW_BLOCK = 512
FF_BLOCK = 1024
POST_ROW_SPLIT = 2
ATTN_BLOCK = 512
ATTN_HEAD_GROUP = 8
ATTN_LOOKAHEAD = 2
GLA_BLOCK = 256
LRU_CHAIN_STAGE = 8
ODD_PROJ_CHUNK = 256
LRU_COL_SLAB = 256


def _rms(x):
    return x * lax.rsqrt(jnp.mean(x * x, axis=-1, keepdims=True) + NORM_EPS)


def _dot(a, b):
    return jnp.dot(a, b, preferred_element_type=F32)


def _dot_nt(a, b):
    return lax.dot_general(a, b, (((1,), (1,)), ((), ())), preferred_element_type=F32)


def _dot_tn(a, b):
    return lax.dot_general(a, b, (((0,), (0,)), ((), ())), preferred_element_type=F32)


def _sigmoid(x):
    return 0.5 * jnp.tanh(0.5 * x) + 0.5


def _softplus(x):
    return jnp.maximum(x, 0.0) + jnp.log1p(jnp.exp(-jnp.abs(x)))


def _params(*semantics, flags=None):
    return pltpu.CompilerParams(dimension_semantics=semantics, vmem_limit_bytes=VMEM_LIMIT_BYTES, flags=flags)


def _causal_conv(tail_ref, x, w_ref, first_block):
    rows, width = x.shape
    taps = w_ref.shape[0]
    groups = rows // SUBLANES

    if first_block is not None:
        @pl.when(first_block)
        def _():
            tail_ref[...] = jnp.zeros((SUBLANES, width), F32)

    x3 = x.reshape(groups, SUBLANES, width)
    with_tail = jnp.concatenate([tail_ref[...][None], x3], axis=0)
    tail_ref[...] = x3[groups - 1]
    sub = lax.broadcasted_iota(jnp.int32, (groups, SUBLANES, width), 1)
    out = w_ref[taps - 1:taps, :] * x3
    for k in range(taps - 1):
        shift = taps - 1 - k
        rolled = pltpu.roll(with_tail, shift, axis=1)
        shifted = jnp.where(sub >= shift, rolled[1:], rolled[:-1])
        out = out + w_ref[k:k + 1, :] * shifted
    return out.reshape(rows, width)


def _rope_table_kernel(pos_ref, freq_ref, sign_ref, cos_ref, sin_ref):
    ang = pos_ref[...].astype(F32) * freq_ref[...]
    cos_ref[...] = jnp.cos(ang)
    sin_ref[...] = jnp.sin(ang) * sign_ref[...]


def _rope_tables(positions):
    b, s = positions.shape
    t = b * s
    inv_freq = ROPE_THETA ** (-jnp.arange(0, MLA_ROPE, 2, dtype=F32) / MLA_ROPE)
    half = MLA_ROPE // 2
    zeros_lo = jnp.zeros((MLA_NOPE,), F32)
    zeros_hi = jnp.zeros((HEAD_PAD - MLA_NOPE - MLA_ROPE,), F32)
    freq = jnp.concatenate([zeros_lo, inv_freq, inv_freq, zeros_hi])[None, :]
    sign = jnp.concatenate([zeros_lo, -jnp.ones((half,), F32), jnp.ones((half,), F32), zeros_hi])[None, :]
    ts = ROW_BLOCK
    row = pl.BlockSpec((ts, 1), lambda i: (i, 0))
    lane = pl.BlockSpec((1, HEAD_PAD), lambda i: (0, 0))
    out = pl.BlockSpec((ts, HEAD_PAD), lambda i: (i, 0))
    return pl.pallas_call(
        _rope_table_kernel,
        grid=(t // ts,),
        in_specs=[row, lane, lane],
        out_specs=[out, out],
        out_shape=[jax.ShapeDtypeStruct((t, HEAD_PAD), F32)] * 2,
        compiler_params=_params("parallel"),
        name="rope_tables",
    )(positions.reshape(t, 1), freq, sign)


def _even_pre_kernel(h_ref, g_ref, win_ref, cw_ref, qg_ref, kvg_ref, wq_ref, wqs_ref, wk_ref, wv_ref,
                     cos_ref, sin_ref, sc_ref, q_ref, k_ref, v_ref, ext_ref):
    hn = (_rms(h_ref[...]) * g_ref[...]).astype(BF16)
    proj = _dot(hn, win_ref[...])

    gate_b = proj[:, EV_GB:EV_GB + SC_WIDTH]
    cu = proj[:, EV_GC:EV_GC + SC_WIDTH] * proj[:, EV_U:EV_U + SC_WIDTH]
    conv = _causal_conv(ext_ref, cu, cw_ref, pl.program_id(1) == 0)
    sc_ref[...] = (gate_b * conv).astype(BF16)

    cos = cos_ref[...]
    sin = sin_ref[...]
    cqn = (_rms(proj[:, EV_CQ:EV_CQ + MLA_Q_RANK]) * qg_ref[...]).astype(BF16)
    q0 = _dot(cqn, wq_ref[...])
    q1 = _dot(cqn, wqs_ref[...])
    ckvn = (_rms(proj[:, EV_CKV:EV_CKV + MLA_KV_RANK]) * kvg_ref[...]).astype(BF16)
    kn = _dot(ckvn, wk_ref[...])
    vt = _dot_nt(wv_ref[...], ckvn)
    feat = lax.broadcasted_iota(jnp.int32, vt.shape, 0)
    ones_feat = (1 - (feat // HEAD_PAD) % 2) * MLA_V
    v_ref[...] = jnp.where(feat % HEAD_PAD == ones_feat, 1.0, vt).astype(BF16)
    k_rope = proj[:, EV_KR:EV_KR + LANES] * cos + proj[:, EV_KRS:EV_KRS + LANES] * sin
    for head in range(MLA_HEADS):
        sl = slice(head * HEAD_PAD, (head + 1) * HEAD_PAD)
        q_ref[:, sl] = ((q0[:, sl] * cos + q1[:, sl] * sin) * QK_SCALE_LOG2E).astype(BF16)
        k_ref[:, sl] = (kn[:, sl] + k_rope).astype(BF16)


def _even_pre(h, norm_g, w, cos, sin, batch, seq):
    t = batch * seq
    tm = ROW_BLOCK
    nblk = seq // tm
    rows = lambda width: pl.BlockSpec((tm, width), lambda b, i: (b * nblk + i, 0))
    full = lambda arr: pl.BlockSpec(arr.shape, lambda b, i: (0,) * arr.ndim)
    consts = [norm_g, w["w_in"], w["conv_w"], w["q_norm_g"], w["kv_norm_g"], w["w_q"], w["w_q_swap"], w["w_k"],
              w["w_v"]]
    return pl.pallas_call(
        _even_pre_kernel,
        grid=(batch, nblk),
        in_specs=[rows(D_MODEL)] + [full(a) for a in consts] + [rows(HEAD_PAD), rows(HEAD_PAD)],
        out_specs=[rows(SC_WIDTH), rows(MLA_PAD), rows(MLA_PAD),
                   pl.BlockSpec((None, None, MLA_PAD, tm), lambda b, i: (b, i, 0, 0))],
        out_shape=[jax.ShapeDtypeStruct((t, SC_WIDTH), BF16)] + [jax.ShapeDtypeStruct((t, MLA_PAD), BF16)] * 2
        + [jax.ShapeDtypeStruct((batch, nblk, MLA_PAD, tm), BF16)],
        scratch_shapes=[pltpu.VMEM((SUBLANES, SC_WIDTH), F32)],
        compiler_params=_params("parallel", "arbitrary"),
        name="even_pre",
    )(h, *consts, cos, sin)


def _attn_kernel(q_ref, k_ref, v_ref, o_ref, acc_ref):
    tq = q_ref.shape[0]
    qi = pl.program_id(1)
    key = lax.broadcasted_iota(jnp.int32, (tq, tq), 0)
    query = lax.broadcasted_iota(jnp.int32, (tq, tq), 1)
    causal = key <= query
    feat = lax.broadcasted_iota(jnp.int32, (HEAD_PAD, tq), 0)

    def scores(head, j, masked):
        sl = slice(head * HEAD_PAD, (head + 1) * HEAD_PAD)
        start = pl.multiple_of(j * tq, tq)
        s = _dot_nt(k_ref[pl.ds(start, tq), sl], q_ref[:, sl])
        return jnp.where(causal, s, -jnp.inf) if masked else s

    def accumulate(head, j, m, s):
        sl = slice(head * HEAD_PAD, (head + 1) * HEAD_PAD)
        m_new = jnp.maximum(m, jnp.max(s, axis=0, keepdims=True))
        p = jnp.exp2(s - m_new)
        alpha = jnp.exp2(m - m_new)
        acc_ref[head] = alpha * acc_ref[head] + _dot(v_ref[j, sl, :], p.astype(BF16))
        return m_new

    def tile(heads, j, ms, masked):
        out = []
        pending = [scores(h, j, masked) for h in heads[:ATTN_LOOKAHEAD]]
        for idx, (h, m) in enumerate(zip(heads, ms)):
            if idx + ATTN_LOOKAHEAD < len(heads):
                pending.append(scores(heads[idx + ATTN_LOOKAHEAD], j, masked))
            out.append(accumulate(h, j, m, pending.pop(0)))
        return tuple(out)

    for group in range(MLA_HEADS // ATTN_HEAD_GROUP):
        heads = tuple(range(group * ATTN_HEAD_GROUP, (group + 1) * ATTN_HEAD_GROUP))
        for h in heads:
            acc_ref[h] = jnp.zeros((HEAD_PAD, tq), F32)
        m0 = jnp.full((1, tq), -jnp.inf, F32)
        ms = lax.fori_loop(0, qi, lambda j, ms: tile(heads, j, ms, False), (m0,) * ATTN_HEAD_GROUP)
        tile(heads, qi, ms, True)
        for even in heads[::2]:
            acc_even = acc_ref[even]
            acc_odd = acc_ref[even + 1]
            out = jnp.where(feat < MLA_V, acc_even / acc_even[MLA_V:MLA_V + 1, :], acc_odd / acc_odd[0:1, :])
            o_ref[:, (even // 2) * LANES:(even // 2 + 1) * LANES] = out.T.astype(BF16)


def _attention(q, k, vt, batch, seq):
    tq = ATTN_BLOCK
    q3, k3 = (a.reshape(batch, seq, MLA_PAD) for a in (q, k))
    out = pl.pallas_call(
        _attn_kernel,
        grid=(batch, seq // tq),
        in_specs=[pl.BlockSpec((None, tq, MLA_PAD), lambda b, i: (b, i, 0)),
                  pl.BlockSpec((None, seq, MLA_PAD), lambda b, i: (b, 0, 0)),
                  pl.BlockSpec((None,) + vt.shape[1:], lambda b, i: (b, 0, 0, 0))],
        out_specs=pl.BlockSpec((None, tq, MLA_HEADS * MLA_V), lambda b, i: (b, i, 0)),
        out_shape=jax.ShapeDtypeStruct((batch, seq, MLA_HEADS * MLA_V), BF16),
        scratch_shapes=[pltpu.VMEM((MLA_HEADS, HEAD_PAD, tq), F32)],
        compiler_params=_params("parallel", "arbitrary"),
        name="mla_attention",
    )(q3, k3, vt)
    return out.reshape(batch * seq, MLA_HEADS * MLA_V)


def _interleave(*stage_generators):
    live = list(stage_generators)
    while live:
        for gen in list(live):
            try:
                next(gen)
            except StopIteration:
                live.remove(gen)


def _gla_stages(q, k, v, g, og, gup_ref, gbias_ref, ng_ref, states, o_ref, rows):
    tb = q.shape[0]
    n_pairs = GLA_HEADS // 2
    pair_dk = 2 * GLA_DK
    pair_dv = 2 * GLA_DV
    n_chunks = tb // GLA_CHUNK

    gate = _dot(g.astype(BF16), gup_ref[...]) + gbias_ref[...]
    log_a = (jnp.minimum(gate, 0.0) - jnp.log1p(jnp.exp(-jnp.abs(gate)))) / GLA_TAU
    yield

    row = lax.broadcasted_iota(jnp.int32, (tb, tb), 0)
    col = lax.broadcasted_iota(jnp.int32, (tb, tb), 1)
    tril = ((row // GLA_CHUNK) == (col // GLA_CHUNK)) & (col <= row)
    tril_bf = tril.astype(BF16)
    cum = None
    rest = log_a
    for _ in range(3):
        piece = rest.astype(BF16)
        rest = rest - piece.astype(F32)
        part = _dot(tril_bf, piece)
        cum = part if cum is None else cum + part
    chunk_last = [cum[(c + 1) * GLA_CHUNK - 1:(c + 1) * GLA_CHUNK, :] for c in range(n_chunks)]
    last = jnp.concatenate([jnp.broadcast_to(t, (GLA_CHUNK, t.shape[1])) for t in chunk_last], axis=0)
    yield

    q_in = (q * (GLA_DK ** -0.5)) * jnp.exp(cum)
    k_in = k * jnp.exp(-cum)
    k_out = k * jnp.exp(last - cum)
    yield

    lane = lax.broadcasted_iota(jnp.int32, (tb, pair_dk), 1)
    state_rows = lax.broadcasted_iota(jnp.int32, (pair_dv, pair_dk), 0)
    state_lanes = lax.broadcasted_iota(jnp.int32, (pair_dv, pair_dk), 1)
    own_block = (state_rows < GLA_DV) == (state_lanes < GLA_DK)

    ksl = [slice(p * pair_dk, (p + 1) * pair_dk) for p in range(n_pairs)]
    vsl = [slice(p * pair_dv, (p + 1) * pair_dv) for p in range(n_pairs)]

    kvs = [[None] * n_chunks for _ in range(n_pairs)]
    for p in range(n_pairs):
        for c in range(n_chunks):
            crow = slice(c * GLA_CHUNK, (c + 1) * GLA_CHUNK)
            kvs[p][c] = jnp.where(own_block, _dot_tn(v[crow, vsl[p]], k_out[crow, ksl[p]].astype(BF16)), 0.0)
        yield
    entering = [[] for _ in range(n_pairs)]
    for c in range(n_chunks):
        for p in range(n_pairs):
            entering[p].append(states[p])
            states[p] = states[p] * jnp.exp(chunk_last[c][:, ksl[p]]) + kvs[p][c]
    yield

    for p in range(n_pairs):
        qp, kp = q_in[:, ksl[p]], k_in[:, ksl[p]]
        o_inter = jnp.concatenate(
            [_dot_nt(qp[c * GLA_CHUNK:(c + 1) * GLA_CHUNK].astype(BF16), entering[p][c].astype(BF16))
             for c in range(n_chunks)], axis=0)
        yield
        for sub in range(2):
            mine = (lane < GLA_DK) if sub == 0 else (lane >= GLA_DK)
            qh = jnp.where(mine, qp, 0.0).astype(BF16)
            kh = jnp.where(mine, kp, 0.0).astype(BF16)
            att = jnp.where(tril, _dot_nt(qh, kh), 0.0)
            head = 2 * p + sub
            hsl = slice(head * GLA_DV, (head + 1) * GLA_DV)
            o = _dot(att.astype(BF16), v[:, hsl]) + o_inter[:, sub * GLA_DV:(sub + 1) * GLA_DV]
            o = _rms(o) * ng_ref[...]
            gate_o = og[:, hsl]
            o_ref[rows, hsl] = (o * (gate_o * _sigmoid(gate_o))).astype(BF16)
            yield


def _lru_stages(x, y, cw_ref, cb_ref, wa_ref, ba_ref, wi_ref, bi_ref, lam_ref, tail_ref, carry_ref, o_ref,
                first_block):
    tb = x.shape[0]
    xc = _causal_conv(tail_ref, x, cw_ref, first_block) + cb_ref[...]
    xcb = xc.astype(BF16)
    yield

    groups = tb // SUBLANES
    sub = lax.broadcasted_iota(jnp.int32, (groups, SUBLANES, LRU_COL_SLAB), 1)
    for start in range(0, LRU_WIDTH, LRU_COL_SLAB):
        cols = slice(start, start + LRU_COL_SLAB)
        r = _sigmoid(_dot(xcb, wa_ref[:, cols]) + ba_ref[:, cols])
        yield
        gate_i = _sigmoid(_dot(xcb, wi_ref[:, cols]) + bi_ref[:, cols])
        yield
        log_a = (-LRU_C * r) * _softplus(-lam_ref[:, cols])
        a = jnp.exp(log_a)
        b = jnp.sqrt(-jnp.tanh(log_a) * (a * a + 1.0)) * (gate_i * xc[:, cols])
        yield

        a = a.reshape(groups, SUBLANES, LRU_COL_SLAB)
        b = b.reshape(groups, SUBLANES, LRU_COL_SLAB)
        shift = 1
        while shift < SUBLANES:
            a_prev = pltpu.roll(a, shift, axis=1)
            b_prev = pltpu.roll(b, shift, axis=1)
            take = sub >= shift
            b = jnp.where(take, a * b_prev + b, b)
            a = jnp.where(take, a * a_prev, a)
            shift *= 2
            yield

        h_prev = carry_ref[:, cols]
        hs = []
        for grp in range(groups):
            h = a[grp] * h_prev + b[grp]
            hs.append(h)
            h_prev = jnp.broadcast_to(h[SUBLANES - 1:SUBLANES, :], (SUBLANES, LRU_COL_SLAB))
            if grp % LRU_CHAIN_STAGE == LRU_CHAIN_STAGE - 1:
                yield
        carry_ref[:, cols] = h_prev
        hseq = jnp.stack(hs, axis=0).reshape(tb, LRU_COL_SLAB)
        ys = y[:, cols]
        gelu = ys * (0.5 * (1.0 + jnp.tanh(0.7978845608028654 * (ys + 0.044715 * (ys * ys * ys)))))
        o_ref[:, cols] = (gelu * hseq).astype(BF16)
        yield


def _odd_pre_kernel(h_ref, g_ref, win_ref, gup_ref, gbias_ref, ng_ref, cw_ref, cb_ref, wa_ref, ba_ref, wi_ref,
                    bi_ref, lam_ref, gla_ref, lru_ref, state_ref, tail_ref, carry_ref, proj_even_ref, proj_odd_ref,
                    *, blocks_per_seq):
    j = pl.program_id(0)
    width_k = GLA_HEADS * GLA_DK
    width_v = GLA_HEADS * GLA_DV

    @pl.when(j == 0)
    def _():
        proj_odd_ref[...] = jnp.zeros(proj_odd_ref.shape, F32)

    @pl.when((j == 0) | ((j - 1) % blocks_per_seq == 0))
    def _():
        state_ref[...] = jnp.zeros(state_ref.shape, F32)
        carry_ref[...] = jnp.zeros(carry_ref.shape, F32)
        tail_ref[...] = jnp.zeros(tail_ref.shape, F32)

    def step(write_ref, read_ref):
        hn = (_rms(h_ref[...]) * g_ref[...]).astype(BF16)

        def project():
            for start in range(0, OD_IN_PAD, ODD_PROJ_CHUNK):
                cols = slice(start, min(start + ODD_PROJ_CHUNK, OD_IN_PAD))
                write_ref[:, cols] = _dot(hn, win_ref[:, cols])
                yield

        states = [state_ref[p] for p in range(GLA_HEADS // 2)]

        gla_blocks = []
        for blk in range(h_ref.shape[0] // GLA_BLOCK):
            rows = slice(blk * GLA_BLOCK, (blk + 1) * GLA_BLOCK)
            gla_blocks.append(_gla_stages(
                read_ref[rows, OD_Q:OD_Q + width_k], read_ref[rows, OD_K:OD_K + width_k],
                read_ref[rows, OD_V:OD_V + width_v].astype(BF16), read_ref[rows, OD_G:OD_G + LANES],
                read_ref[rows, OD_OG:OD_OG + width_v], gup_ref, gbias_ref, ng_ref, states, gla_ref, rows))
        lru = _lru_stages(read_ref[:, OD_X:OD_X + LRU_WIDTH], read_ref[:, OD_Y:OD_Y + LRU_WIDTH], cw_ref, cb_ref,
                          wa_ref, ba_ref, wi_ref, bi_ref, lam_ref, tail_ref, carry_ref, lru_ref, None)
        _interleave(project(), lru, *gla_blocks)
        for p, s in enumerate(states):
            state_ref[p] = s

    @pl.when(j % 2 == 0)
    def _():
        step(proj_even_ref, proj_odd_ref)

    @pl.when(j % 2 == 1)
    def _():
        step(proj_odd_ref, proj_even_ref)


def _odd_pre(h, norm_g, w, batch, seq):
    t = batch * seq
    tm = ROW_BLOCK
    nblk = seq // tm
    n_steps = t // tm + 1
    full = lambda arr: pl.BlockSpec(arr.shape, lambda j: (0,) * arr.ndim)
    consts = [norm_g, w["w_in"], w["gate_up"], w["gate_bias"], w["gla_norm_g"], w["conv_w"], w["conv_b"], w["w_a"],
              w["b_a"], w["w_i"], w["b_i"], w["lam"]]
    width_v = GLA_HEADS * GLA_DV
    block_in = pl.BlockSpec((tm, D_MODEL), lambda j: (jnp.minimum(j, n_steps - 2), 0))
    block_out = lambda width: pl.BlockSpec((tm, width), lambda j: (jnp.maximum(j - 1, 0), 0))
    return pl.pallas_call(
        functools.partial(_odd_pre_kernel, blocks_per_seq=nblk),
        grid=(n_steps,),
        in_specs=[block_in] + [full(a) for a in consts],
        out_specs=[block_out(width_v), block_out(LRU_WIDTH)],
        out_shape=[jax.ShapeDtypeStruct((t, width_v), BF16), jax.ShapeDtypeStruct((t, LRU_WIDTH), BF16)],
        scratch_shapes=[pltpu.VMEM((GLA_HEADS // 2, 2 * GLA_DV, 2 * GLA_DK), F32),
                        pltpu.VMEM((SUBLANES, LRU_WIDTH), F32),
                        pltpu.VMEM((SUBLANES, LRU_WIDTH), F32),
                        pltpu.VMEM((tm, OD_IN_PAD), F32),
                        pltpu.VMEM((tm, OD_IN_PAD), F32)],
        compiler_params=_params("arbitrary"),
        name="odd_pre",
    )(h, *consts)


def _post_kernel(h_ref, a_ref, b_ref, wo_ref, g_ref, w1_ref, w2_ref, gf_ref, o_ref, *, final_norm):
    half = a_ref.shape[1]
    sub = h_ref.shape[0] // POST_ROW_SPLIT
    for r in range(POST_ROW_SPLIT):
        rows = slice(r * sub, (r + 1) * sub)
        mix = _dot(a_ref[rows, :], wo_ref[0:half, :]) + _dot(b_ref[rows, :], wo_ref[half:2 * half, :])
        h1 = h_ref[rows, :] + mix
        hn = (_rms(h1) * g_ref[...]).astype(BF16)
        out = h1
        for c in range(D_FF // FF_BLOCK):
            cols = slice(c * FF_BLOCK, (c + 1) * FF_BLOCK)
            hidden = _dot(hn, w1_ref[:, cols])
            act = jnp.square(jnp.maximum(hidden, 0.0)).astype(BF16)
            out = out + _dot(act, w2_ref[cols, :])
        if final_norm:
            out = _rms(out) * gf_ref[...]
        o_ref[rows, :] = out


def _post(h, a, b, w_out, norm_g, w1, w2, final_g, final_norm):
    t = h.shape[0]
    tm = ROW_BLOCK
    rows = lambda width: pl.BlockSpec((tm, width), lambda i: (i, 0))
    vec = pl.BlockSpec((1, D_MODEL), lambda i: (0, 0))
    resident = lambda arr: pl.BlockSpec(arr.shape, lambda i: (0, 0), pipeline_mode=pl.Buffered(1))
    return pl.pallas_call(
        functools.partial(_post_kernel, final_norm=final_norm),
        grid=(t // tm,),
        in_specs=[rows(D_MODEL), rows(a.shape[1]), rows(b.shape[1]), resident(w_out), vec, resident(w1),
                  resident(w2), vec],
        out_specs=rows(D_MODEL),
        out_shape=jax.ShapeDtypeStruct((t, D_MODEL), F32),
        compiler_params=_params("parallel"),
        name="post_mlp",
    )(h, a, b, w_out, norm_g, w1, w2, final_g)


def _even_weights(w_in, conv_w, q_norm_g, w_uq, kv_norm_g, w_ukv):
    half = MLA_ROPE // 2
    z = lambda *shape: jnp.zeros(shape, F32)
    kr = w_in[:, EV_KR:EV_KR + MLA_ROPE]
    kr1, kr2 = kr[:, :half], kr[:, half:]
    tail = HEAD_PAD - MLA_NOPE - MLA_ROPE
    kr_blk = jnp.concatenate([z(D_MODEL, MLA_NOPE), kr1, kr2, z(D_MODEL, tail)], axis=1)
    krs_blk = jnp.concatenate([z(D_MODEL, MLA_NOPE), kr2, kr1, z(D_MODEL, tail)], axis=1)
    w_in_pad = jnp.concatenate([w_in[:, :EV_KR], kr_blk, krs_blk], axis=1)

    wq = w_uq.reshape(MLA_Q_RANK, MLA_HEADS, MLA_NOPE + MLA_ROPE)
    nope, r1, r2 = wq[..., :MLA_NOPE], wq[..., MLA_NOPE:MLA_NOPE + half], wq[..., MLA_NOPE + half:]
    zq = z(MLA_Q_RANK, MLA_HEADS, tail)
    w_q = jnp.concatenate([nope, r1, r2, zq], axis=-1).reshape(MLA_Q_RANK, MLA_PAD)
    w_q_swap = jnp.concatenate([jnp.zeros_like(nope), r2, r1, zq], axis=-1).reshape(MLA_Q_RANK, MLA_PAD)

    wkv = w_ukv.reshape(MLA_KV_RANK, MLA_HEADS, MLA_NOPE + MLA_V)
    k_nope, v = wkv[..., :MLA_NOPE], wkv[..., MLA_NOPE:]
    w_k = jnp.concatenate([k_nope, z(MLA_KV_RANK, MLA_HEADS, HEAD_PAD - MLA_NOPE)], axis=-1)
    zv = jnp.zeros_like(v)
    odd_head = (jnp.arange(MLA_HEADS) % 2 == 1)[None, :, None]
    w_v = jnp.concatenate([jnp.where(odd_head, zv, v), jnp.where(odd_head, v, zv)], axis=-1)
    return {
        "w_in": w_in_pad.astype(BF16), "conv_w": conv_w, "q_norm_g": q_norm_g[None, :],
        "kv_norm_g": kv_norm_g[None, :], "w_q": w_q.astype(BF16), "w_q_swap": w_q_swap.astype(BF16),
        "w_k": w_k.reshape(MLA_KV_RANK, MLA_PAD).astype(BF16),
        "w_v": w_v.reshape(MLA_KV_RANK, MLA_PAD).T.astype(BF16),
    }


def _block_diag(w):
    n, r, c = w.shape
    eye = jnp.eye(n, dtype=w.dtype)
    return (eye[:, None, :, None] * w[:, :, None, :]).reshape(n * r, n * c)


def _odd_weights(w_in, gate_up, gate_bias, gla_norm_g, conv_w, conv_b, w_a, b_a, w_i, b_i, lam):
    dk, dv = GLA_HEADS * GLA_DK, GLA_HEADS * GLA_DV
    o = 0
    parts = {}
    for name, width in (("q", dk), ("k", dk), ("v", dv), ("g", GLA_GATE_RANK), ("og", dv), ("x", LRU_WIDTH),
                        ("y", LRU_WIDTH)):
        parts[name] = w_in[:, o:o + width]
        o += width
    g_pad = jnp.zeros((D_MODEL, LANES - GLA_GATE_RANK), F32)
    w_in_pad = jnp.concatenate([parts["x"], parts["y"], parts["v"], parts["og"], parts["q"], parts["k"], parts["g"],
                                g_pad], axis=1)
    gate_up_pad = jnp.concatenate([gate_up, jnp.zeros((LANES - GLA_GATE_RANK, dk), F32)], axis=0)
    return {
        "w_in": w_in_pad.astype(BF16), "gate_up": gate_up_pad.astype(BF16), "gate_bias": gate_bias[None, :],
        "gla_norm_g": gla_norm_g[None, :], "conv_w": conv_w, "conv_b": conv_b[None, :],
        "w_a": _block_diag(w_a).astype(BF16), "b_a": b_a[None, :], "w_i": _block_diag(w_i).astype(BF16),
        "b_i": b_i[None, :], "lam": lam[None, :],
    }


def kernel(x, positions, mixer_norm_g, mlp_norm_g, mlp_w1, mlp_w2, final_norm_g, ev_w_in, ev_conv_w, mla_q_norm_g, mla_w_uq, mla_kv_norm_g, mla_w_ukv, ev_w_out, od_w_in, gla_w_gate_up, gla_b_gate, gla_norm_g, lru_conv_w, lru_conv_b, lru_w_a, lru_b_a, lru_w_i, lru_b_i, lru_lambda, od_w_out):
    batch, seq, d_model = x.shape
    depth = mixer_norm_g.shape[0]
    assert d_model == D_MODEL and seq % ROW_BLOCK == 0 and ROW_BLOCK % GLA_BLOCK == 0
    assert ATTN_BLOCK == ROW_BLOCK
    h = x.reshape(batch * seq, D_MODEL)
    cos, sin = _rope_tables(positions)
    final_g = final_norm_g[None, :]
    for layer in range(depth):
        j = layer // 2
        norm_g = mixer_norm_g[layer][None, :]
        if layer % 2 == 0:
            w = _even_weights(ev_w_in[j], ev_conv_w[j], mla_q_norm_g[j], mla_w_uq[j], mla_kv_norm_g[j], mla_w_ukv[j])
            mix_a, q, k, v = _even_pre(h, norm_g, w, cos, sin, batch, seq)
            mix_b = _attention(q, k, v, batch, seq)
            w_out = ev_w_out[j]
        else:
            w = _odd_weights(od_w_in[j], gla_w_gate_up[j], gla_b_gate[j], gla_norm_g[j], lru_conv_w[j],
                             lru_conv_b[j], lru_w_a[j], lru_b_a[j], lru_w_i[j], lru_b_i[j], lru_lambda[j])
            mix_a, mix_b = _odd_pre(h, norm_g, w, batch, seq)
            w_out = od_w_out[j]
        h = _post(h, mix_a, mix_b, w_out.astype(BF16), mlp_norm_g[layer][None, :], mlp_w1[layer].astype(BF16),
                  mlp_w2[layer].astype(BF16), final_g, final_norm=(layer == depth - 1))
    return h.reshape(batch, seq, D_MODEL)
```

```python
import functools

import jax
import jax.numpy as jnp
from jax import lax
from jax.experimental import pallas as pl
from jax.experimental.pallas import tpu as pltpu

F32 = jnp.float32
BF16 = jnp.bfloat16

D_MODEL = 1024
D_FF = 4 * D_MODEL
NORM_EPS = 1e-6

SC_WIDTH = 512
SC_KERNEL = 3
MLA_HEADS = 8
MLA_NOPE = 64
MLA_ROPE = 32
MLA_V = 64
MLA_Q_RANK = 384
MLA_KV_RANK = 256
ROPE_THETA = 10000.0
QK_SCALE_LOG2E = (MLA_NOPE + MLA_ROPE) ** -0.5 * 1.4426950408889634

GLA_HEADS = 4
GLA_DK = 64
GLA_DV = 128
GLA_GATE_RANK = 16
GLA_TAU = 16.0
GLA_CHUNK = 64
LRU_WIDTH = 512
LRU_BLOCKS = 8
LRU_KERNEL = 4
LRU_C = 8.0

LANES = 128
SUBLANES = 8
VMEM_LIMIT_BYTES = 56 * 1024 * 1024

HEAD_PAD = LANES
ROPE_SWAP_ROLL = LANES - MLA_ROPE // 2
MLA_PAD = MLA_HEADS * HEAD_PAD

EV_GB, EV_GC, EV_U = 0, 512, 1024
EV_CQ = 1536
EV_CKV = EV_CQ + MLA_Q_RANK
EV_KR = EV_CKV + MLA_KV_RANK
EV_IN_PAD = EV_KR + LANES

OD_X, OD_Y, OD_V, OD_OG = 0, 512, 1024, 1536
OD_Q, OD_K, OD_G = 2048, 2304, 2560
OD_IN_PAD = OD_G + LANES

ROW_BLOCK = 512
FF_BLOCK = 1024
POST_ROW_SPLIT = 2
ATTN_BLOCK = 512
ATTN_HEAD_GROUP = 8
ATTN_LOOKAHEAD = 2
GLA_BLOCK = 256
LRU_CHAIN_STAGE = 8
ODD_PROJ_CHUNK = 256
EVEN_PROJ_CHUNK = 256
LRU_COL_SLAB = 256


def _rms(x):
    return x * lax.rsqrt(jnp.mean(x * x, axis=-1, keepdims=True) + NORM_EPS)


def _dot(a, b):
    return jnp.dot(a, b, preferred_element_type=F32)


def _dot_nt(a, b):
    return lax.dot_general(a, b, (((1,), (1,)), ((), ())), preferred_element_type=F32)


def _dot_tn(a, b):
    return lax.dot_general(a, b, (((0,), (0,)), ((), ())), preferred_element_type=F32)


def _sigmoid(x):
    return 0.5 * jnp.tanh(0.5 * x) + 0.5


def _softplus(x):
    return jnp.maximum(x, 0.0) + jnp.log1p(jnp.exp(-jnp.abs(x)))


def _params(*semantics, flags=None):
    return pltpu.CompilerParams(dimension_semantics=semantics, vmem_limit_bytes=VMEM_LIMIT_BYTES, flags=flags)


def _causal_conv(tail_ref, x, w_ref, first_block):
    rows, width = x.shape
    taps = w_ref.shape[0]
    groups = rows // SUBLANES

    if first_block is not None:
        @pl.when(first_block)
        def _():
            tail_ref[...] = jnp.zeros((SUBLANES, width), F32)

    x3 = x.reshape(groups, SUBLANES, width)
    with_tail = jnp.concatenate([tail_ref[...][None], x3], axis=0)
    tail_ref[...] = x3[groups - 1]
    sub = lax.broadcasted_iota(jnp.int32, (groups, SUBLANES, width), 1)
    out = w_ref[taps - 1:taps, :] * x3
    for k in range(taps - 1):
        shift = taps - 1 - k
        rolled = pltpu.roll(with_tail, shift, axis=1)
        shifted = jnp.where(sub >= shift, rolled[1:], rolled[:-1])
        out = out + w_ref[k:k + 1, :] * shifted
    return out.reshape(rows, width)


def _rope_table_kernel(pos_ref, freq_ref, sign_ref, keep_ref, cos_ref, sin_ref):
    ang = pos_ref[...].astype(F32) * freq_ref[...]
    cos_ref[...] = jnp.cos(ang) * keep_ref[...]
    sin_ref[...] = jnp.sin(ang) * sign_ref[...]


def _rope_tables(positions):
    b, s = positions.shape
    t = b * s
    inv_freq = ROPE_THETA ** (-jnp.arange(0, MLA_ROPE, 2, dtype=F32) / MLA_ROPE)
    half = MLA_ROPE // 2
    zeros_lo = jnp.zeros((MLA_NOPE,), F32)
    zeros_hi = jnp.zeros((HEAD_PAD - MLA_NOPE - MLA_ROPE,), F32)
    freq = jnp.concatenate([zeros_lo, inv_freq, inv_freq, zeros_hi])[None, :]
    sign = jnp.concatenate([zeros_lo, -jnp.ones((half,), F32), jnp.ones((half,), F32), zeros_hi])[None, :]
    keep = jnp.concatenate([jnp.ones((MLA_NOPE + MLA_ROPE,), F32), zeros_hi])[None, :]
    ts = ROW_BLOCK
    row = pl.BlockSpec((ts, 1), lambda i: (i, 0))
    lane = pl.BlockSpec((1, HEAD_PAD), lambda i: (0, 0))
    out = pl.BlockSpec((ts, HEAD_PAD), lambda i: (i, 0))
    return pl.pallas_call(
        _rope_table_kernel,
        grid=(t // ts,),
        in_specs=[row, lane, lane, lane],
        out_specs=[out, out],
        out_shape=[jax.ShapeDtypeStruct((t, HEAD_PAD), F32)] * 2,
        compiler_params=_params("parallel"),
        name="rope_tables",
    )(positions.reshape(t, 1), freq, sign, keep)


def _even_pre_kernel(h_ref, g_ref, win_ref, cw_ref, qg_ref, kvg_ref, wq_ref, wk_ref, wv_ref, cos_ref, sin_ref,
                     sc_ref, q_ref, k_ref, v_ref, tail_ref, proj_even_ref, proj_odd_ref, *, blocks_per_seq):
    j = pl.program_id(0)

    @pl.when(j == 0)
    def _():
        proj_odd_ref[...] = jnp.zeros(proj_odd_ref.shape, F32)

    @pl.when((j == 0) | ((j - 1) % blocks_per_seq == 0))
    def _():
        tail_ref[...] = jnp.zeros(tail_ref.shape, F32)

    def step(write_ref, read_ref):
        hn = (_rms(h_ref[...]) * g_ref[...]).astype(BF16)

        def project():
            for start in range(0, EV_IN_PAD, EVEN_PROJ_CHUNK):
                cols = slice(start, min(start + EVEN_PROJ_CHUNK, EV_IN_PAD))
                write_ref[:, cols] = _dot(hn, win_ref[:, cols])
                yield

        def finish():
            gate_b = read_ref[:, EV_GB:EV_GB + SC_WIDTH]
            cu = read_ref[:, EV_GC:EV_GC + SC_WIDTH] * read_ref[:, EV_U:EV_U + SC_WIDTH]
            sc_ref[...] = (gate_b * _causal_conv(tail_ref, cu, cw_ref, None)).astype(BF16)
            yield
            cqn = (_rms(read_ref[:, EV_CQ:EV_CQ + MLA_Q_RANK]) * qg_ref[...]).astype(BF16)
            q0 = _dot(cqn, wq_ref[...])
            yield
            ckvn = (_rms(read_ref[:, EV_CKV:EV_CKV + MLA_KV_RANK]) * kvg_ref[...]).astype(BF16)
            kn = _dot(ckvn, wk_ref[...])
            yield
            vt = _dot_nt(wv_ref[...], ckvn)
            feat = lax.broadcasted_iota(jnp.int32, vt.shape, 0)
            ones_feat = (1 - (feat // HEAD_PAD) % 2) * MLA_V
            v_ref[...] = jnp.where(feat % HEAD_PAD == ones_feat, 1.0, vt).astype(BF16)
            yield
            cos = cos_ref[...]
            sin = sin_ref[...]
            rope = lambda x: x * cos + pltpu.roll(x, ROPE_SWAP_ROLL, axis=1) * sin
            k_rope = rope(read_ref[:, EV_KR:EV_KR + LANES])
            for head in range(MLA_HEADS):
                sl = slice(head * HEAD_PAD, (head + 1) * HEAD_PAD)
                q_ref[:, sl] = (rope(q0[:, sl]) * QK_SCALE_LOG2E).astype(BF16)
                k_ref[:, sl] = (kn[:, sl] + k_rope).astype(BF16)
                yield

        _interleave(project(), finish())

    @pl.when(j % 2 == 0)
    def _():
        step(proj_even_ref, proj_odd_ref)

    @pl.when(j % 2 == 1)
    def _():
        step(proj_odd_ref, proj_even_ref)


def _even_pre(h, norm_g, w, cos, sin, batch, seq):
    t = batch * seq
    tm = ROW_BLOCK
    nblk = seq // tm
    n_steps = t // tm + 1
    full = lambda arr: pl.BlockSpec(arr.shape, lambda j: (0,) * arr.ndim)
    consts = [norm_g, w["w_in"], w["conv_w"], w["q_norm_g"], w["kv_norm_g"], w["w_q"], w["w_k"], w["w_v"]]
    prev = lambda j: jnp.maximum(j - 1, 0)
    block_in = pl.BlockSpec((tm, D_MODEL), lambda j: (jnp.minimum(j, n_steps - 2), 0))
    rows_prev = lambda width: pl.BlockSpec((tm, width), lambda j: (prev(j), 0))
    return pl.pallas_call(
        functools.partial(_even_pre_kernel, blocks_per_seq=nblk),
        grid=(n_steps,),
        in_specs=[block_in] + [full(a) for a in consts] + [rows_prev(HEAD_PAD), rows_prev(HEAD_PAD)],
        out_specs=[rows_prev(SC_WIDTH), rows_prev(MLA_PAD), rows_prev(MLA_PAD),
                   pl.BlockSpec((None, None, MLA_PAD, tm), lambda j: (prev(j) // nblk, prev(j) % nblk, 0, 0))],
        out_shape=[jax.ShapeDtypeStruct((t, SC_WIDTH), BF16)] + [jax.ShapeDtypeStruct((t, MLA_PAD), BF16)] * 2
        + [jax.ShapeDtypeStruct((batch, nblk, MLA_PAD, tm), BF16)],
        scratch_shapes=[pltpu.VMEM((SUBLANES, SC_WIDTH), F32),
                        pltpu.VMEM((tm, EV_IN_PAD), F32),
                        pltpu.VMEM((tm, EV_IN_PAD), F32)],
        compiler_params=_params("arbitrary"),
        name="even_pre",
    )(h, *consts, cos, sin)


def _attn_kernel(q_ref, k_ref, v_ref, o_ref, acc_ref):
    tq = q_ref.shape[0]
    qi = pl.program_id(1)
    key = lax.broadcasted_iota(jnp.int32, (tq, tq), 0)
    query = lax.broadcasted_iota(jnp.int32, (tq, tq), 1)
    causal = key <= query
    feat = lax.broadcasted_iota(jnp.int32, (HEAD_PAD, tq), 0)

    def scores(head, j, masked):
        sl = slice(head * HEAD_PAD, (head + 1) * HEAD_PAD)
        start = pl.multiple_of(j * tq, tq)
        s = _dot_nt(k_ref[pl.ds(start, tq), sl], q_ref[:, sl])
        return jnp.where(causal, s, -jnp.inf) if masked else s

    def accumulate(head, j, m, s):
        sl = slice(head * HEAD_PAD, (head + 1) * HEAD_PAD)
        m_new = jnp.maximum(m, jnp.max(s, axis=0, keepdims=True))
        p = jnp.exp2(s - m_new)
        alpha = jnp.exp2(m - m_new)
        acc_ref[head] = alpha * acc_ref[head] + _dot(v_ref[j, sl, :], p.astype(BF16))
        return m_new

    def tile(heads, j, ms, masked):
        out = []
        pending = [scores(h, j, masked) for h in heads[:ATTN_LOOKAHEAD]]
        for idx, (h, m) in enumerate(zip(heads, ms)):
            if idx + ATTN_LOOKAHEAD < len(heads):
                pending.append(scores(heads[idx + ATTN_LOOKAHEAD], j, masked))
            out.append(accumulate(h, j, m, pending.pop(0)))
        return tuple(out)

    for group in range(MLA_HEADS // ATTN_HEAD_GROUP):
        heads = tuple(range(group * ATTN_HEAD_GROUP, (group + 1) * ATTN_HEAD_GROUP))
        for h in heads:
            acc_ref[h] = jnp.zeros((HEAD_PAD, tq), F32)
        m0 = jnp.full((1, tq), -jnp.inf, F32)
        ms = lax.fori_loop(0, qi, lambda j, ms: tile(heads, j, ms, False), (m0,) * ATTN_HEAD_GROUP)
        tile(heads, qi, ms, True)
        for even in heads[::2]:
            acc_even = acc_ref[even]
            acc_odd = acc_ref[even + 1]
            out = jnp.where(feat < MLA_V, acc_even / acc_even[MLA_V:MLA_V + 1, :], acc_odd / acc_odd[0:1, :])
            o_ref[:, (even // 2) * LANES:(even // 2 + 1) * LANES] = out.T.astype(BF16)


def _attention(q, k, vt, batch, seq):
    tq = ATTN_BLOCK
    q3, k3 = (a.reshape(batch, seq, MLA_PAD) for a in (q, k))
    out = pl.pallas_call(
        _attn_kernel,
        grid=(batch, seq // tq),
        in_specs=[pl.BlockSpec((None, tq, MLA_PAD), lambda b, i: (b, i, 0)),
                  pl.BlockSpec((None, seq, MLA_PAD), lambda b, i: (b, 0, 0)),
                  pl.BlockSpec((None,) + vt.shape[1:], lambda b, i: (b, 0, 0, 0))],
        out_specs=pl.BlockSpec((None, tq, MLA_HEADS * MLA_V), lambda b, i: (b, i, 0)),
        out_shape=jax.ShapeDtypeStruct((batch, seq, MLA_HEADS * MLA_V), BF16),
        scratch_shapes=[pltpu.VMEM((MLA_HEADS, HEAD_PAD, tq), F32)],
        compiler_params=_params("parallel", "arbitrary"),
        name="mla_attention",
    )(q3, k3, vt)
    return out.reshape(batch * seq, MLA_HEADS * MLA_V)


def _interleave(*stage_generators):
    live = list(stage_generators)
    while live:
        for gen in list(live):
            try:
                next(gen)
            except StopIteration:
                live.remove(gen)


def _gla_stages(q, k, v, g, og, gup_ref, gbias_ref, ng_ref, states, o_ref, rows):
    tb = q.shape[0]
    n_pairs = GLA_HEADS // 2
    pair_dk = 2 * GLA_DK
    pair_dv = 2 * GLA_DV
    n_chunks = tb // GLA_CHUNK

    gate = _dot(g.astype(BF16), gup_ref[...]) + gbias_ref[...]
    log_a = (jnp.minimum(gate, 0.0) - jnp.log1p(jnp.exp(-jnp.abs(gate)))) / GLA_TAU
    yield

    row = lax.broadcasted_iota(jnp.int32, (tb, tb), 0)
    col = lax.broadcasted_iota(jnp.int32, (tb, tb), 1)
    tril = ((row // GLA_CHUNK) == (col // GLA_CHUNK)) & (col <= row)
    tril_bf = tril.astype(BF16)
    cum = None
    rest = log_a
    for _ in range(3):
        piece = rest.astype(BF16)
        rest = rest - piece.astype(F32)
        part = _dot(tril_bf, piece)
        cum = part if cum is None else cum + part
    chunk_last = [cum[(c + 1) * GLA_CHUNK - 1:(c + 1) * GLA_CHUNK, :] for c in range(n_chunks)]
    last = jnp.concatenate([jnp.broadcast_to(t, (GLA_CHUNK, t.shape[1])) for t in chunk_last], axis=0)
    yield

    q_in = (q * (GLA_DK ** -0.5)) * jnp.exp(cum)
    k_in = k * jnp.exp(-cum)
    k_out = k * jnp.exp(last - cum)
    yield

    lane = lax.broadcasted_iota(jnp.int32, (tb, pair_dk), 1)
    state_rows = lax.broadcasted_iota(jnp.int32, (pair_dv, pair_dk), 0)
    state_lanes = lax.broadcasted_iota(jnp.int32, (pair_dv, pair_dk), 1)
    own_block = (state_rows < GLA_DV) == (state_lanes < GLA_DK)

    ksl = [slice(p * pair_dk, (p + 1) * pair_dk) for p in range(n_pairs)]
    vsl = [slice(p * pair_dv, (p + 1) * pair_dv) for p in range(n_pairs)]

    kvs = [[None] * n_chunks for _ in range(n_pairs)]
    for p in range(n_pairs):
        for c in range(n_chunks):
            crow = slice(c * GLA_CHUNK, (c + 1) * GLA_CHUNK)
            kvs[p][c] = jnp.where(own_block, _dot_tn(v[crow, vsl[p]], k_out[crow, ksl[p]].astype(BF16)), 0.0)
        yield
    entering = [[] for _ in range(n_pairs)]
    for c in range(n_chunks):
        for p in range(n_pairs):
            entering[p].append(states[p])
            states[p] = states[p] * jnp.exp(chunk_last[c][:, ksl[p]]) + kvs[p][c]
    yield

    for p in range(n_pairs):
        qp, kp = q_in[:, ksl[p]], k_in[:, ksl[p]]
        o_inter = jnp.concatenate(
            [_dot_nt(qp[c * GLA_CHUNK:(c + 1) * GLA_CHUNK].astype(BF16), entering[p][c].astype(BF16))
             for c in range(n_chunks)], axis=0)
        yield
        for sub in range(2):
            mine = (lane < GLA_DK) if sub == 0 else (lane >= GLA_DK)
            qh = jnp.where(mine, qp, 0.0).astype(BF16)
            kh = jnp.where(mine, kp, 0.0).astype(BF16)
            att = jnp.where(tril, _dot_nt(qh, kh), 0.0)
            head = 2 * p + sub
            hsl = slice(head * GLA_DV, (head + 1) * GLA_DV)
            o = _dot(att.astype(BF16), v[:, hsl]) + o_inter[:, sub * GLA_DV:(sub + 1) * GLA_DV]
            o = _rms(o) * ng_ref[...]
            gate_o = og[:, hsl]
            o_ref[rows, hsl] = (o * (gate_o * _sigmoid(gate_o))).astype(BF16)
            yield


def _lru_stages(x, y, cw_ref, cb_ref, wa_ref, ba_ref, wi_ref, bi_ref, lam_ref, tail_ref, carry_ref, o_ref,
                first_block):
    tb = x.shape[0]
    xc = _causal_conv(tail_ref, x, cw_ref, first_block) + cb_ref[...]
    xcb = xc.astype(BF16)
    yield

    groups = tb // SUBLANES
    sub = lax.broadcasted_iota(jnp.int32, (groups, SUBLANES, LRU_COL_SLAB), 1)
    for start in range(0, LRU_WIDTH, LRU_COL_SLAB):
        cols = slice(start, start + LRU_COL_SLAB)
        r = _sigmoid(_dot(xcb, wa_ref[:, cols]) + ba_ref[:, cols])
        yield
        gate_i = _sigmoid(_dot(xcb, wi_ref[:, cols]) + bi_ref[:, cols])
        yield
        log_a = (-LRU_C * r) * _softplus(-lam_ref[:, cols])
        a = jnp.exp(log_a)
        b = jnp.sqrt(-jnp.tanh(log_a) * (a * a + 1.0)) * (gate_i * xc[:, cols])
        yield

        a = a.reshape(groups, SUBLANES, LRU_COL_SLAB)
        b = b.reshape(groups, SUBLANES, LRU_COL_SLAB)
        shift = 1
        while shift < SUBLANES:
            a_prev = pltpu.roll(a, shift, axis=1)
            b_prev = pltpu.roll(b, shift, axis=1)
            take = sub >= shift
            b = jnp.where(take, a * b_prev + b, b)
            a = jnp.where(take, a * a_prev, a)
            shift *= 2
            yield

        h_prev = carry_ref[:, cols]
        hs = []
        for grp in range(groups):
            h = a[grp] * h_prev + b[grp]
            hs.append(h)
            h_prev = jnp.broadcast_to(h[SUBLANES - 1:SUBLANES, :], (SUBLANES, LRU_COL_SLAB))
            if grp % LRU_CHAIN_STAGE == LRU_CHAIN_STAGE - 1:
                yield
        carry_ref[:, cols] = h_prev
        hseq = jnp.stack(hs, axis=0).reshape(tb, LRU_COL_SLAB)
        ys = y[:, cols]
        gelu = ys * (0.5 * (1.0 + jnp.tanh(0.7978845608028654 * (ys + 0.044715 * (ys * ys * ys)))))
        o_ref[:, cols] = (gelu * hseq).astype(BF16)
        yield


def _odd_pre_kernel(h_ref, g_ref, win_ref, gup_ref, gbias_ref, ng_ref, cw_ref, cb_ref, wa_ref, ba_ref, wi_ref,
                    bi_ref, lam_ref, gla_ref, lru_ref, state_ref, tail_ref, carry_ref, proj_even_ref, proj_odd_ref,
                    *, blocks_per_seq):
    j = pl.program_id(0)
    width_k = GLA_HEADS * GLA_DK
    width_v = GLA_HEADS * GLA_DV

    @pl.when(j == 0)
    def _():
        proj_odd_ref[...] = jnp.zeros(proj_odd_ref.shape, F32)

    @pl.when((j == 0) | ((j - 1) % blocks_per_seq == 0))
    def _():
        state_ref[...] = jnp.zeros(state_ref.shape, F32)
        carry_ref[...] = jnp.zeros(carry_ref.shape, F32)
        tail_ref[...] = jnp.zeros(tail_ref.shape, F32)

    def step(write_ref, read_ref):
        hn = (_rms(h_ref[...]) * g_ref[...]).astype(BF16)

        def project():
            for start in range(0, OD_IN_PAD, ODD_PROJ_CHUNK):
                cols = slice(start, min(start + ODD_PROJ_CHUNK, OD_IN_PAD))
                write_ref[:, cols] = _dot(hn, win_ref[:, cols])
                yield

        states = [state_ref[p] for p in range(GLA_HEADS // 2)]

        gla_blocks = []
        for blk in range(h_ref.shape[0] // GLA_BLOCK):
            rows = slice(blk * GLA_BLOCK, (blk + 1) * GLA_BLOCK)
            gla_blocks.append(_gla_stages(
                read_ref[rows, OD_Q:OD_Q + width_k], read_ref[rows, OD_K:OD_K + width_k],
                read_ref[rows, OD_V:OD_V + width_v].astype(BF16), read_ref[rows, OD_G:OD_G + LANES],
                read_ref[rows, OD_OG:OD_OG + width_v], gup_ref, gbias_ref, ng_ref, states, gla_ref, rows))
        lru = _lru_stages(read_ref[:, OD_X:OD_X + LRU_WIDTH], read_ref[:, OD_Y:OD_Y + LRU_WIDTH], cw_ref, cb_ref,
                          wa_ref, ba_ref, wi_ref, bi_ref, lam_ref, tail_ref, carry_ref, lru_ref, None)
        _interleave(project(), lru, *gla_blocks)
        for p, s in enumerate(states):
            state_ref[p] = s

    @pl.when(j % 2 == 0)
    def _():
        step(proj_even_ref, proj_odd_ref)

    @pl.when(j % 2 == 1)
    def _():
        step(proj_odd_ref, proj_even_ref)


def _odd_pre(h, norm_g, w, batch, seq):
    t = batch * seq
    tm = ROW_BLOCK
    nblk = seq // tm
    n_steps = t // tm + 1
    full = lambda arr: pl.BlockSpec(arr.shape, lambda j: (0,) * arr.ndim)
    consts = [norm_g, w["w_in"], w["gate_up"], w["gate_bias"], w["gla_norm_g"], w["conv_w"], w["conv_b"], w["w_a"],
              w["b_a"], w["w_i"], w["b_i"], w["lam"]]
    width_v = GLA_HEADS * GLA_DV
    block_in = pl.BlockSpec((tm, D_MODEL), lambda j: (jnp.minimum(j, n_steps - 2), 0))
    block_out = lambda width: pl.BlockSpec((tm, width), lambda j: (jnp.maximum(j - 1, 0), 0))
    return pl.pallas_call(
        functools.partial(_odd_pre_kernel, blocks_per_seq=nblk),
        grid=(n_steps,),
        in_specs=[block_in] + [full(a) for a in consts],
        out_specs=[block_out(width_v), block_out(LRU_WIDTH)],
        out_shape=[jax.ShapeDtypeStruct((t, width_v), BF16), jax.ShapeDtypeStruct((t, LRU_WIDTH), BF16)],
        scratch_shapes=[pltpu.VMEM((GLA_HEADS // 2, 2 * GLA_DV, 2 * GLA_DK), F32),
                        pltpu.VMEM((SUBLANES, LRU_WIDTH), F32),
                        pltpu.VMEM((SUBLANES, LRU_WIDTH), F32),
                        pltpu.VMEM((tm, OD_IN_PAD), F32),
                        pltpu.VMEM((tm, OD_IN_PAD), F32)],
        compiler_params=_params("arbitrary"),
        name="odd_pre",
    )(h, *consts)


def _post_kernel(h_ref, a_ref, b_ref, wo_ref, g_ref, w1_ref, w2_ref, gf_ref, o_ref, *, final_norm):
    half = a_ref.shape[1]
    sub = h_ref.shape[0] // POST_ROW_SPLIT
    for r in range(POST_ROW_SPLIT):
        rows = slice(r * sub, (r + 1) * sub)
        mix = _dot(a_ref[rows, :], wo_ref[0:half, :]) + _dot(b_ref[rows, :], wo_ref[half:2 * half, :])
        h1 = h_ref[rows, :] + mix
        hn = (_rms(h1) * g_ref[...]).astype(BF16)
        out = h1
        for c in range(D_FF // FF_BLOCK):
            cols = slice(c * FF_BLOCK, (c + 1) * FF_BLOCK)
            hidden = _dot(hn, w1_ref[:, cols])
            act = jnp.square(jnp.maximum(hidden, 0.0)).astype(BF16)
            out = out + _dot(act, w2_ref[cols, :])
        if final_norm:
            out = _rms(out) * gf_ref[...]
        o_ref[rows, :] = out


def _post(h, a, b, w_out, norm_g, w1, w2, final_g, final_norm):
    t = h.shape[0]
    tm = ROW_BLOCK
    rows = lambda width: pl.BlockSpec((tm, width), lambda i: (i, 0))
    vec = pl.BlockSpec((1, D_MODEL), lambda i: (0, 0))
    resident = lambda arr: pl.BlockSpec(arr.shape, lambda i: (0, 0), pipeline_mode=pl.Buffered(1))
    return pl.pallas_call(
        functools.partial(_post_kernel, final_norm=final_norm),
        grid=(t // tm,),
        in_specs=[rows(D_MODEL), rows(a.shape[1]), rows(b.shape[1]), resident(w_out), vec, resident(w1),
                  resident(w2), vec],
        out_specs=rows(D_MODEL),
        out_shape=jax.ShapeDtypeStruct((t, D_MODEL), F32),
        compiler_params=_params("parallel"),
        name="post_mlp",
    )(h, a, b, w_out, norm_g, w1, w2, final_g)


def _even_weights(w_in, conv_w, q_norm_g, w_uq, kv_norm_g, w_ukv):
    half = MLA_ROPE // 2
    z = lambda *shape: jnp.zeros(shape, F32)
    kr = w_in[:, EV_KR:EV_KR + MLA_ROPE]
    kr1, kr2 = kr[:, :half], kr[:, half:]
    tail = HEAD_PAD - MLA_NOPE - MLA_ROPE - half
    kr_blk = jnp.concatenate([z(D_MODEL, MLA_NOPE), kr1, kr2, kr1, z(D_MODEL, tail)], axis=1)
    w_in_pad = jnp.concatenate([w_in[:, :EV_KR], kr_blk], axis=1)

    wq = w_uq.reshape(MLA_Q_RANK, MLA_HEADS, MLA_NOPE + MLA_ROPE)
    nope, r1, r2 = wq[..., :MLA_NOPE], wq[..., MLA_NOPE:MLA_NOPE + half], wq[..., MLA_NOPE + half:]
    zq = z(MLA_Q_RANK, MLA_HEADS, tail)
    w_q = jnp.concatenate([nope, r1, r2, r1, zq], axis=-1).reshape(MLA_Q_RANK, MLA_PAD)

    wkv = w_ukv.reshape(MLA_KV_RANK, MLA_HEADS, MLA_NOPE + MLA_V)
    k_nope, v = wkv[..., :MLA_NOPE], wkv[..., MLA_NOPE:]
    w_k = jnp.concatenate([k_nope, z(MLA_KV_RANK, MLA_HEADS, HEAD_PAD - MLA_NOPE)], axis=-1)
    zv = jnp.zeros_like(v)
    odd_head = (jnp.arange(MLA_HEADS) % 2 == 1)[None, :, None]
    w_v = jnp.concatenate([jnp.where(odd_head, zv, v), jnp.where(odd_head, v, zv)], axis=-1)
    return {
        "w_in": w_in_pad.astype(BF16), "conv_w": conv_w, "q_norm_g": q_norm_g[None, :],
        "kv_norm_g": kv_norm_g[None, :], "w_q": w_q.astype(BF16),
        "w_k": w_k.reshape(MLA_KV_RANK, MLA_PAD).astype(BF16),
        "w_v": w_v.reshape(MLA_KV_RANK, MLA_PAD).T.astype(BF16),
    }


def _block_diag(w):
    n, r, c = w.shape
    eye = jnp.eye(n, dtype=w.dtype)
    return (eye[:, None, :, None] * w[:, :, None, :]).reshape(n * r, n * c)


def _odd_weights(w_in, gate_up, gate_bias, gla_norm_g, conv_w, conv_b, w_a, b_a, w_i, b_i, lam):
    dk, dv = GLA_HEADS * GLA_DK, GLA_HEADS * GLA_DV
    o = 0
    parts = {}
    for name, width in (("q", dk), ("k", dk), ("v", dv), ("g", GLA_GATE_RANK), ("og", dv), ("x", LRU_WIDTH),
                        ("y", LRU_WIDTH)):
        parts[name] = w_in[:, o:o + width]
        o += width
    g_pad = jnp.zeros((D_MODEL, LANES - GLA_GATE_RANK), F32)
    w_in_pad = jnp.concatenate([parts["x"], parts["y"], parts["v"], parts["og"], parts["q"], parts["k"], parts["g"],
                                g_pad], axis=1)
    gate_up_pad = jnp.concatenate([gate_up, jnp.zeros((LANES - GLA_GATE_RANK, dk), F32)], axis=0)
    return {
        "w_in": w_in_pad.astype(BF16), "gate_up": gate_up_pad.astype(BF16), "gate_bias": gate_bias[None, :],
        "gla_norm_g": gla_norm_g[None, :], "conv_w": conv_w, "conv_b": conv_b[None, :],
        "w_a": _block_diag(w_a).astype(BF16), "b_a": b_a[None, :], "w_i": _block_diag(w_i).astype(BF16),
        "b_i": b_i[None, :], "lam": lam[None, :],
    }


def kernel(x, positions, mixer_norm_g, mlp_norm_g, mlp_w1, mlp_w2, final_norm_g, ev_w_in, ev_conv_w, mla_q_norm_g, mla_w_uq, mla_kv_norm_g, mla_w_ukv, ev_w_out, od_w_in, gla_w_gate_up, gla_b_gate, gla_norm_g, lru_conv_w, lru_conv_b, lru_w_a, lru_b_a, lru_w_i, lru_b_i, lru_lambda, od_w_out):
    batch, seq, d_model = x.shape
    depth = mixer_norm_g.shape[0]
    assert d_model == D_MODEL and seq % ROW_BLOCK == 0 and ROW_BLOCK % GLA_BLOCK == 0
    assert ATTN_BLOCK == ROW_BLOCK
    h = x.reshape(batch * seq, D_MODEL)
    cos, sin = _rope_tables(positions)
    final_g = final_norm_g[None, :]
    for layer in range(depth):
        j = layer // 2
        norm_g = mixer_norm_g[layer][None, :]
        if layer % 2 == 0:
            w = _even_weights(ev_w_in[j], ev_conv_w[j], mla_q_norm_g[j], mla_w_uq[j], mla_kv_norm_g[j], mla_w_ukv[j])
            mix_a, q, k, v = _even_pre(h, norm_g, w, cos, sin, batch, seq)
            mix_b = _attention(q, k, v, batch, seq)
            w_out = ev_w_out[j]
        else:
            w = _odd_weights(od_w_in[j], gla_w_gate_up[j], gla_b_gate[j], gla_norm_g[j], lru_conv_w[j],
                             lru_conv_b[j], lru_w_a[j], lru_b_a[j], lru_w_i[j], lru_b_i[j], lru_lambda[j])
            mix_a, mix_b = _odd_pre(h, norm_g, w, batch, seq)
            w_out = od_w_out[j]
        h = _post(h, mix_a, mix_b, w_out.astype(BF16), mlp_norm_g[layer][None, :], mlp_w1[layer].astype(BF16),
                  mlp_w2[layer].astype(BF16), final_g, final_norm=(layer == depth - 1))
    return h.reshape(batch, seq, D_MODEL)
```

```python
import functools

import jax
import jax.numpy as jnp
from jax import lax
from jax.experimental import pallas as pl
from jax.experimental.pallas import tpu as pltpu

F32 = jnp.float32
BF16 = jnp.bfloat16

D_MODEL = 1024
D_FF = 4 * D_MODEL
NORM_EPS = 1e-6

SC_WIDTH = 512
SC_KERNEL = 3
MLA_HEADS = 8
MLA_NOPE = 64
MLA_ROPE = 32
MLA_V = 64
MLA_Q_RANK = 384
MLA_KV_RANK = 256
ROPE_THETA = 10000.0
QK_SCALE_LOG2E = (MLA_NOPE + MLA_ROPE) ** -0.5 * 1.4426950408889634

GLA_HEADS = 4
GLA_DK = 64
GLA_DV = 128
GLA_GATE_RANK = 16
GLA_TAU = 16.0
GLA_CHUNK = 64
LRU_WIDTH = 512
LRU_BLOCKS = 8
LRU_KERNEL = 4
LRU_C = 8.0

LANES = 128
SUBLANES = 8
VMEM_LIMIT_BYTES = 56 * 1024 * 1024

HEAD_PAD = LANES
ROPE_SWAP_ROLL = LANES - MLA_ROPE // 2
MLA_PAD = MLA_HEADS * HEAD_PAD

EV_GB, EV_GC, EV_U = 0, 512, 1024
EV_CQ = 1536
EV_CKV = EV_CQ + MLA_Q_RANK
EV_KR = EV_CKV + MLA_KV_RANK
EV_IN_PAD = EV_KR + LANES

OD_X, OD_Y, OD_V, OD_OG = 0, 512, 1024, 1536
OD_Q, OD_K, OD_G = 2048, 2304, 2560
OD_IN_PAD = OD_G + LANES

ROW_BLOCK = 512
FF_BLOCK = 1024
POST_ROW_BLOCK = 1024
POST_ROW_SPLIT = 4
ATTN_BLOCK = 512
ATTN_HEAD_GROUP = 8
ATTN_LOOKAHEAD = 2
GLA_BLOCK = 256
LRU_CHAIN_STAGE = 8
ODD_PROJ_CHUNK = 512
EVEN_PROJ_CHUNK = 256
LRU_COL_SLAB = 512


def _rms(x):
    return x * lax.rsqrt(jnp.mean(x * x, axis=-1, keepdims=True) + NORM_EPS)


def _dot(a, b):
    return jnp.dot(a, b, preferred_element_type=F32)


def _dot_nt(a, b):
    return lax.dot_general(a, b, (((1,), (1,)), ((), ())), preferred_element_type=F32)


def _dot_tn(a, b):
    return lax.dot_general(a, b, (((0,), (0,)), ((), ())), preferred_element_type=F32)


def _sigmoid(x):
    return 0.5 * jnp.tanh(0.5 * x) + 0.5


def _softplus(x):
    return jnp.maximum(x, 0.0) + jnp.log1p(jnp.exp(-jnp.abs(x)))


def _params(*semantics, flags=None):
    return pltpu.CompilerParams(dimension_semantics=semantics, vmem_limit_bytes=VMEM_LIMIT_BYTES, flags=flags)


def _causal_conv(tail_ref, x, w_ref, first_block):
    rows, width = x.shape
    taps = w_ref.shape[0]
    groups = rows // SUBLANES

    if first_block is not None:
        @pl.when(first_block)
        def _():
            tail_ref[...] = jnp.zeros((SUBLANES, width), F32)

    x3 = x.reshape(groups, SUBLANES, width)
    with_tail = jnp.concatenate([tail_ref[...][None], x3], axis=0)
    tail_ref[...] = x3[groups - 1]
    sub = lax.broadcasted_iota(jnp.int32, (groups, SUBLANES, width), 1)
    out = w_ref[taps - 1:taps, :] * x3
    for k in range(taps - 1):
        shift = taps - 1 - k
        rolled = pltpu.roll(with_tail, shift, axis=1)
        shifted = jnp.where(sub >= shift, rolled[1:], rolled[:-1])
        out = out + w_ref[k:k + 1, :] * shifted
    return out.reshape(rows, width)


def _rope_table_kernel(pos_ref, freq_ref, sign_ref, keep_ref, cos_ref, sin_ref):
    ang = pos_ref[...].astype(F32) * freq_ref[...]
    cos_ref[...] = jnp.cos(ang) * keep_ref[...]
    sin_ref[...] = jnp.sin(ang) * sign_ref[...]


def _rope_tables(positions):
    b, s = positions.shape
    t = b * s
    inv_freq = ROPE_THETA ** (-jnp.arange(0, MLA_ROPE, 2, dtype=F32) / MLA_ROPE)
    half = MLA_ROPE // 2
    zeros_lo = jnp.zeros((MLA_NOPE,), F32)
    zeros_hi = jnp.zeros((HEAD_PAD - MLA_NOPE - MLA_ROPE,), F32)
    freq = jnp.concatenate([zeros_lo, inv_freq, inv_freq, zeros_hi])[None, :]
    sign = jnp.concatenate([zeros_lo, -jnp.ones((half,), F32), jnp.ones((half,), F32), zeros_hi])[None, :]
    keep = jnp.concatenate([jnp.ones((MLA_NOPE + MLA_ROPE,), F32), zeros_hi])[None, :]
    ts = ROW_BLOCK
    row = pl.BlockSpec((ts, 1), lambda i: (i, 0))
    lane = pl.BlockSpec((1, HEAD_PAD), lambda i: (0, 0))
    out = pl.BlockSpec((ts, HEAD_PAD), lambda i: (i, 0))
    return pl.pallas_call(
        _rope_table_kernel,
        grid=(t // ts,),
        in_specs=[row, lane, lane, lane],
        out_specs=[out, out],
        out_shape=[jax.ShapeDtypeStruct((t, HEAD_PAD), F32)] * 2,
        compiler_params=_params("parallel"),
        name="rope_tables",
    )(positions.reshape(t, 1), freq, sign, keep)


def _even_pre_kernel(h_ref, g_ref, win_ref, cw_ref, qg_ref, kvg_ref, wq_ref, wk_ref, wv_ref, cos_ref, sin_ref,
                     sc_ref, q_ref, k_ref, v_ref, tail_ref, proj_even_ref, proj_odd_ref, *, blocks_per_seq):
    j = pl.program_id(0)

    @pl.when(j == 0)
    def _():
        proj_odd_ref[...] = jnp.zeros(proj_odd_ref.shape, F32)

    @pl.when((j == 0) | ((j - 1) % blocks_per_seq == 0))
    def _():
        tail_ref[...] = jnp.zeros(tail_ref.shape, F32)

    def step(write_ref, read_ref):
        hn = (_rms(h_ref[...]) * g_ref[...]).astype(BF16)

        def project():
            for start in range(0, EV_IN_PAD, EVEN_PROJ_CHUNK):
                cols = slice(start, min(start + EVEN_PROJ_CHUNK, EV_IN_PAD))
                write_ref[:, cols] = _dot(hn, win_ref[:, cols])
                yield

        def finish():
            gate_b = read_ref[:, EV_GB:EV_GB + SC_WIDTH]
            cu = read_ref[:, EV_GC:EV_GC + SC_WIDTH] * read_ref[:, EV_U:EV_U + SC_WIDTH]
            sc_ref[...] = (gate_b * _causal_conv(tail_ref, cu, cw_ref, None)).astype(BF16)
            yield
            cqn = (_rms(read_ref[:, EV_CQ:EV_CQ + MLA_Q_RANK]) * qg_ref[...]).astype(BF16)
            q0 = _dot(cqn, wq_ref[...])
            yield
            ckvn = (_rms(read_ref[:, EV_CKV:EV_CKV + MLA_KV_RANK]) * kvg_ref[...]).astype(BF16)
            kn = _dot(ckvn, wk_ref[...])
            yield
            vt = _dot_nt(wv_ref[...], ckvn)
            feat = lax.broadcasted_iota(jnp.int32, vt.shape, 0)
            ones_feat = (1 - (feat // HEAD_PAD) % 2) * MLA_V
            v_ref[...] = jnp.where(feat % HEAD_PAD == ones_feat, 1.0, vt).astype(BF16)
            yield
            cos = cos_ref[...]
            sin = sin_ref[...]
            rope = lambda x: x * cos + pltpu.roll(x, ROPE_SWAP_ROLL, axis=1) * sin
            k_rope = rope(read_ref[:, EV_KR:EV_KR + LANES])
            for head in range(MLA_HEADS):
                sl = slice(head * HEAD_PAD, (head + 1) * HEAD_PAD)
                q_ref[:, sl] = (rope(q0[:, sl]) * QK_SCALE_LOG2E).astype(BF16)
                k_ref[:, sl] = (kn[:, sl] + k_rope).astype(BF16)
                yield

        _interleave(project(), finish())

    @pl.when(j % 2 == 0)
    def _():
        step(proj_even_ref, proj_odd_ref)

    @pl.when(j % 2 == 1)
    def _():
        step(proj_odd_ref, proj_even_ref)


def _even_pre(h, norm_g, w, cos, sin, batch, seq):
    t = batch * seq
    tm = ROW_BLOCK
    nblk = seq // tm
    n_steps = t // tm + 1
    full = lambda arr: pl.BlockSpec(arr.shape, lambda j: (0,) * arr.ndim)
    consts = [norm_g, w["w_in"], w["conv_w"], w["q_norm_g"], w["kv_norm_g"], w["w_q"], w["w_k"], w["w_v"]]
    prev = lambda j: jnp.maximum(j - 1, 0)
    block_in = pl.BlockSpec((tm, D_MODEL), lambda j: (jnp.minimum(j, n_steps - 2), 0))
    rows_prev = lambda width: pl.BlockSpec((tm, width), lambda j: (prev(j), 0))
    return pl.pallas_call(
        functools.partial(_even_pre_kernel, blocks_per_seq=nblk),
        grid=(n_steps,),
        in_specs=[block_in] + [full(a) for a in consts] + [rows_prev(HEAD_PAD), rows_prev(HEAD_PAD)],
        out_specs=[rows_prev(SC_WIDTH), rows_prev(MLA_PAD), rows_prev(MLA_PAD),
                   pl.BlockSpec((None, None, MLA_PAD, tm), lambda j: (prev(j) // nblk, prev(j) % nblk, 0, 0))],
        out_shape=[jax.ShapeDtypeStruct((t, SC_WIDTH), BF16)] + [jax.ShapeDtypeStruct((t, MLA_PAD), BF16)] * 2
        + [jax.ShapeDtypeStruct((batch, nblk, MLA_PAD, tm), BF16)],
        scratch_shapes=[pltpu.VMEM((SUBLANES, SC_WIDTH), F32),
                        pltpu.VMEM((tm, EV_IN_PAD), F32),
                        pltpu.VMEM((tm, EV_IN_PAD), F32)],
        compiler_params=_params("arbitrary"),
        name="even_pre",
    )(h, *consts, cos, sin)


def _attn_kernel(q_ref, k_ref, v_ref, o_ref, acc_ref):
    tq = q_ref.shape[0]
    qi = pl.program_id(1)
    key = lax.broadcasted_iota(jnp.int32, (tq, tq), 0)
    query = lax.broadcasted_iota(jnp.int32, (tq, tq), 1)
    causal = key <= query
    feat = lax.broadcasted_iota(jnp.int32, (HEAD_PAD, tq), 0)

    def scores(head, j, masked):
        sl = slice(head * HEAD_PAD, (head + 1) * HEAD_PAD)
        start = pl.multiple_of(j * tq, tq)
        s = _dot_nt(k_ref[pl.ds(start, tq), sl], q_ref[:, sl])
        return jnp.where(causal, s, -jnp.inf) if masked else s

    def accumulate(head, j, m, s):
        sl = slice(head * HEAD_PAD, (head + 1) * HEAD_PAD)
        m_new = jnp.maximum(m, jnp.max(s, axis=0, keepdims=True))
        p = jnp.exp2(s - m_new)
        alpha = jnp.exp2(m - m_new)
        acc_ref[head] = alpha * acc_ref[head] + _dot(v_ref[j, sl, :], p.astype(BF16))
        return m_new

    def run(heads, tiles, ms):
        ms = list(ms)
        work = [(j, masked, idx) for j, masked in tiles for idx in range(len(heads))]
        pending = [scores(heads[idx], j, masked) for j, masked, idx in work[:ATTN_LOOKAHEAD]]
        for n, (j, masked, idx) in enumerate(work):
            if n + ATTN_LOOKAHEAD < len(work):
                j2, masked2, idx2 = work[n + ATTN_LOOKAHEAD]
                pending.append(scores(heads[idx2], j2, masked2))
            ms[idx] = accumulate(heads[idx], j, ms[idx], pending.pop(0))
        return tuple(ms)

    for group in range(MLA_HEADS // ATTN_HEAD_GROUP):
        heads = tuple(range(group * ATTN_HEAD_GROUP, (group + 1) * ATTN_HEAD_GROUP))
        for h in heads:
            acc_ref[h] = jnp.zeros((HEAD_PAD, tq), F32)
        m0 = jnp.full((1, tq), -jnp.inf, F32)
        ms = lax.fori_loop(0, qi // 2, lambda jj, ms: run(heads, [(2 * jj, False), (2 * jj + 1, False)], ms),
                           (m0,) * ATTN_HEAD_GROUP)

        @pl.when(qi % 2 == 1)
        def _():
            run(heads, [(qi - 1, False), (qi, True)], ms)

        @pl.when(qi % 2 == 0)
        def _():
            run(heads, [(qi, True)], ms)

        for even in heads[::2]:
            acc_even = acc_ref[even]
            acc_odd = acc_ref[even + 1]
            out = jnp.where(feat < MLA_V, acc_even / acc_even[MLA_V:MLA_V + 1, :], acc_odd / acc_odd[0:1, :])
            o_ref[:, (even // 2) * LANES:(even // 2 + 1) * LANES] = out.T.astype(BF16)


def _attention(q, k, vt, batch, seq):
    tq = ATTN_BLOCK
    q3, k3 = (a.reshape(batch, seq, MLA_PAD) for a in (q, k))
    out = pl.pallas_call(
        _attn_kernel,
        grid=(batch, seq // tq),
        in_specs=[pl.BlockSpec((None, tq, MLA_PAD), lambda b, i: (b, i, 0)),
                  pl.BlockSpec((None, seq, MLA_PAD), lambda b, i: (b, 0, 0)),
                  pl.BlockSpec((None,) + vt.shape[1:], lambda b, i: (b, 0, 0, 0))],
        out_specs=pl.BlockSpec((None, tq, MLA_HEADS * MLA_V), lambda b, i: (b, i, 0)),
        out_shape=jax.ShapeDtypeStruct((batch, seq, MLA_HEADS * MLA_V), BF16),
        scratch_shapes=[pltpu.VMEM((MLA_HEADS, HEAD_PAD, tq), F32)],
        compiler_params=_params("parallel", "arbitrary"),
        name="mla_attention",
    )(q3, k3, vt)
    return out.reshape(batch * seq, MLA_HEADS * MLA_V)


def _interleave(*stage_generators):
    live = list(stage_generators)
    while live:
        for gen in list(live):
            try:
                next(gen)
            except StopIteration:
                live.remove(gen)


def _gla_stages(q, k, v, g, og, gup_ref, gbias_ref, ng_ref, states, o_ref, rows):
    tb = q.shape[0]
    n_pairs = GLA_HEADS // 2
    pair_dk = 2 * GLA_DK
    pair_dv = 2 * GLA_DV
    n_chunks = tb // GLA_CHUNK

    gate = _dot(g.astype(BF16), gup_ref[...]) + gbias_ref[...]
    log_a = (jnp.minimum(gate, 0.0) - jnp.log1p(jnp.exp(-jnp.abs(gate)))) / GLA_TAU
    yield

    row = lax.broadcasted_iota(jnp.int32, (tb, tb), 0)
    col = lax.broadcasted_iota(jnp.int32, (tb, tb), 1)
    tril = ((row // GLA_CHUNK) == (col // GLA_CHUNK)) & (col <= row)
    tril_bf = tril.astype(BF16)
    cum = None
    rest = log_a
    for _ in range(3):
        piece = rest.astype(BF16)
        rest = rest - piece.astype(F32)
        part = _dot(tril_bf, piece)
        cum = part if cum is None else cum + part
    chunk_last = [cum[(c + 1) * GLA_CHUNK - 1:(c + 1) * GLA_CHUNK, :] for c in range(n_chunks)]
    last = jnp.concatenate([jnp.broadcast_to(t, (GLA_CHUNK, t.shape[1])) for t in chunk_last], axis=0)
    yield

    q_in = (q * (GLA_DK ** -0.5)) * jnp.exp(cum)
    k_in = k * jnp.exp(-cum)
    k_out = k * jnp.exp(last - cum)
    yield

    lane = lax.broadcasted_iota(jnp.int32, (tb, pair_dk), 1)
    state_rows = lax.broadcasted_iota(jnp.int32, (pair_dv, pair_dk), 0)
    state_lanes = lax.broadcasted_iota(jnp.int32, (pair_dv, pair_dk), 1)
    own_block = (state_rows < GLA_DV) == (state_lanes < GLA_DK)

    ksl = [slice(p * pair_dk, (p + 1) * pair_dk) for p in range(n_pairs)]
    vsl = [slice(p * pair_dv, (p + 1) * pair_dv) for p in range(n_pairs)]

    kvs = [[None] * n_chunks for _ in range(n_pairs)]
    for p in range(n_pairs):
        for c in range(n_chunks):
            crow = slice(c * GLA_CHUNK, (c + 1) * GLA_CHUNK)
            kvs[p][c] = jnp.where(own_block, _dot_tn(v[crow, vsl[p]], k_out[crow, ksl[p]].astype(BF16)), 0.0)
        yield
    entering = [[] for _ in range(n_pairs)]
    for c in range(n_chunks):
        for p in range(n_pairs):
            entering[p].append(states[p])
            states[p] = states[p] * jnp.exp(chunk_last[c][:, ksl[p]]) + kvs[p][c]
    yield

    for p in range(n_pairs):
        qp, kp = q_in[:, ksl[p]], k_in[:, ksl[p]]
        o_inter = jnp.concatenate(
            [_dot_nt(qp[c * GLA_CHUNK:(c + 1) * GLA_CHUNK].astype(BF16), entering[p][c].astype(BF16))
             for c in range(n_chunks)], axis=0)
        yield
        for sub in range(2):
            mine = (lane < GLA_DK) if sub == 0 else (lane >= GLA_DK)
            qh = jnp.where(mine, qp, 0.0).astype(BF16)
            kh = jnp.where(mine, kp, 0.0).astype(BF16)
            att = jnp.where(tril, _dot_nt(qh, kh), 0.0)
            head = 2 * p + sub
            hsl = slice(head * GLA_DV, (head + 1) * GLA_DV)
            o = _dot(att.astype(BF16), v[:, hsl]) + o_inter[:, sub * GLA_DV:(sub + 1) * GLA_DV]
            o = _rms(o) * ng_ref[...]
            gate_o = og[:, hsl]
            o_ref[rows, hsl] = (o * (gate_o * _sigmoid(gate_o))).astype(BF16)
            yield


def _lru_stages(x, y, cw_ref, cb_ref, wa_ref, ba_ref, wi_ref, bi_ref, lam_ref, tail_ref, carry_ref, o_ref,
                first_block):
    tb = x.shape[0]
    xc = _causal_conv(tail_ref, x, cw_ref, first_block) + cb_ref[...]
    xcb = xc.astype(BF16)
    yield

    groups = tb // SUBLANES
    sub = lax.broadcasted_iota(jnp.int32, (groups, SUBLANES, LRU_COL_SLAB), 1)
    for start in range(0, LRU_WIDTH, LRU_COL_SLAB):
        cols = slice(start, start + LRU_COL_SLAB)
        r = _sigmoid(_dot(xcb, wa_ref[:, cols]) + ba_ref[:, cols])
        yield
        gate_i = _sigmoid(_dot(xcb, wi_ref[:, cols]) + bi_ref[:, cols])
        yield
        log_a = (-LRU_C * r) * _softplus(-lam_ref[:, cols])
        a = jnp.exp(log_a)
        b = jnp.sqrt(-jnp.tanh(log_a) * (a * a + 1.0)) * (gate_i * xc[:, cols])
        yield

        a = a.reshape(groups, SUBLANES, LRU_COL_SLAB)
        b = b.reshape(groups, SUBLANES, LRU_COL_SLAB)
        shift = 1
        while shift < SUBLANES:
            a_prev = pltpu.roll(a, shift, axis=1)
            b_prev = pltpu.roll(b, shift, axis=1)
            take = sub >= shift
            b = jnp.where(take, a * b_prev + b, b)
            a = jnp.where(take, a * a_prev, a)
            shift *= 2
            yield

        h_prev = carry_ref[:, cols]
        hs = []
        for grp in range(groups):
            h = a[grp] * h_prev + b[grp]
            hs.append(h)
            h_prev = jnp.broadcast_to(h[SUBLANES - 1:SUBLANES, :], (SUBLANES, LRU_COL_SLAB))
            if grp % LRU_CHAIN_STAGE == LRU_CHAIN_STAGE - 1:
                yield
        carry_ref[:, cols] = h_prev
        hseq = jnp.stack(hs, axis=0).reshape(tb, LRU_COL_SLAB)
        ys = y[:, cols]
        gelu = ys * (0.5 * (1.0 + jnp.tanh(0.7978845608028654 * (ys + 0.044715 * (ys * ys * ys)))))
        o_ref[:, cols] = (gelu * hseq).astype(BF16)
        yield


def _odd_pre_kernel(h_ref, g_ref, win_ref, gup_ref, gbias_ref, ng_ref, cw_ref, cb_ref, wa_ref, ba_ref, wi_ref,
                    bi_ref, lam_ref, gla_ref, lru_ref, state_ref, tail_ref, carry_ref, proj_even_ref, proj_odd_ref,
                    *, blocks_per_seq):
    j = pl.program_id(0)
    width_k = GLA_HEADS * GLA_DK
    width_v = GLA_HEADS * GLA_DV

    @pl.when(j == 0)
    def _():
        proj_odd_ref[...] = jnp.zeros(proj_odd_ref.shape, F32)

    @pl.when((j == 0) | ((j - 1) % blocks_per_seq == 0))
    def _():
        state_ref[...] = jnp.zeros(state_ref.shape, F32)
        carry_ref[...] = jnp.zeros(carry_ref.shape, F32)
        tail_ref[...] = jnp.zeros(tail_ref.shape, F32)

    def step(write_ref, read_ref):
        hn = (_rms(h_ref[...]) * g_ref[...]).astype(BF16)

        def project():
            for start in range(0, OD_IN_PAD, ODD_PROJ_CHUNK):
                cols = slice(start, min(start + ODD_PROJ_CHUNK, OD_IN_PAD))
                write_ref[:, cols] = _dot(hn, win_ref[:, cols])
                yield

        states = [state_ref[p] for p in range(GLA_HEADS // 2)]

        gla_blocks = []
        for blk in range(h_ref.shape[0] // GLA_BLOCK):
            rows = slice(blk * GLA_BLOCK, (blk + 1) * GLA_BLOCK)
            gla_blocks.append(_gla_stages(
                read_ref[rows, OD_Q:OD_Q + width_k], read_ref[rows, OD_K:OD_K + width_k],
                read_ref[rows, OD_V:OD_V + width_v].astype(BF16), read_ref[rows, OD_G:OD_G + LANES],
                read_ref[rows, OD_OG:OD_OG + width_v], gup_ref, gbias_ref, ng_ref, states, gla_ref, rows))
        lru = _lru_stages(read_ref[:, OD_X:OD_X + LRU_WIDTH], read_ref[:, OD_Y:OD_Y + LRU_WIDTH], cw_ref, cb_ref,
                          wa_ref, ba_ref, wi_ref, bi_ref, lam_ref, tail_ref, carry_ref, lru_ref, None)
        _interleave(project(), lru, *gla_blocks)
        for p, s in enumerate(states):
            state_ref[p] = s

    @pl.when(j % 2 == 0)
    def _():
        step(proj_even_ref, proj_odd_ref)

    @pl.when(j % 2 == 1)
    def _():
        step(proj_odd_ref, proj_even_ref)


def _odd_pre(h, norm_g, w, batch, seq):
    t = batch * seq
    tm = ROW_BLOCK
    nblk = seq // tm
    n_steps = t // tm + 1
    full = lambda arr: pl.BlockSpec(arr.shape, lambda j: (0,) * arr.ndim)
    consts = [norm_g, w["w_in"], w["gate_up"], w["gate_bias"], w["gla_norm_g"], w["conv_w"], w["conv_b"], w["w_a"],
              w["b_a"], w["w_i"], w["b_i"], w["lam"]]
    width_v = GLA_HEADS * GLA_DV
    block_in = pl.BlockSpec((tm, D_MODEL), lambda j: (jnp.minimum(j, n_steps - 2), 0))
    block_out = lambda width: pl.BlockSpec((tm, width), lambda j: (jnp.maximum(j - 1, 0), 0))
    return pl.pallas_call(
        functools.partial(_odd_pre_kernel, blocks_per_seq=nblk),
        grid=(n_steps,),
        in_specs=[block_in] + [full(a) for a in consts],
        out_specs=[block_out(width_v), block_out(LRU_WIDTH)],
        out_shape=[jax.ShapeDtypeStruct((t, width_v), BF16), jax.ShapeDtypeStruct((t, LRU_WIDTH), BF16)],
        scratch_shapes=[pltpu.VMEM((GLA_HEADS // 2, 2 * GLA_DV, 2 * GLA_DK), F32),
                        pltpu.VMEM((SUBLANES, LRU_WIDTH), F32),
                        pltpu.VMEM((SUBLANES, LRU_WIDTH), F32),
                        pltpu.VMEM((tm, OD_IN_PAD), F32),
                        pltpu.VMEM((tm, OD_IN_PAD), F32)],
        compiler_params=_params("arbitrary"),
        name="odd_pre",
    )(h, *consts)


def _post_kernel(h_ref, a_ref, b_ref, wo_ref, g_ref, w1_ref, w2_ref, gf_ref, o_ref, *, final_norm):
    half = a_ref.shape[1]
    sub = h_ref.shape[0] // POST_ROW_SPLIT
    for r in range(POST_ROW_SPLIT):
        rows = slice(r * sub, (r + 1) * sub)
        mix = _dot(a_ref[rows, :], wo_ref[0:half, :]) + _dot(b_ref[rows, :], wo_ref[half:2 * half, :])
        h1 = h_ref[rows, :] + mix
        hn = (_rms(h1) * g_ref[...]).astype(BF16)
        out = h1
        for c in range(D_FF // FF_BLOCK):
            cols = slice(c * FF_BLOCK, (c + 1) * FF_BLOCK)
            hidden = _dot(hn, w1_ref[:, cols])
            act = jnp.square(jnp.maximum(hidden, 0.0)).astype(BF16)
            out = out + _dot(act, w2_ref[cols, :])
        if final_norm:
            out = _rms(out) * gf_ref[...]
        o_ref[rows, :] = out


def _post(h, a, b, w_out, norm_g, w1, w2, final_g, final_norm):
    t = h.shape[0]
    tm = POST_ROW_BLOCK
    rows = lambda width: pl.BlockSpec((tm, width), lambda i: (i, 0))
    vec = pl.BlockSpec((1, D_MODEL), lambda i: (0, 0))
    resident = lambda arr: pl.BlockSpec(arr.shape, lambda i: (0, 0), pipeline_mode=pl.Buffered(1))
    return pl.pallas_call(
        functools.partial(_post_kernel, final_norm=final_norm),
        grid=(t // tm,),
        in_specs=[rows(D_MODEL), rows(a.shape[1]), rows(b.shape[1]), resident(w_out), vec, resident(w1),
                  resident(w2), vec],
        out_specs=rows(D_MODEL),
        out_shape=jax.ShapeDtypeStruct((t, D_MODEL), F32),
        compiler_params=_params("parallel"),
        name="post_mlp",
    )(h, a, b, w_out, norm_g, w1, w2, final_g)


def _even_weights(w_in, conv_w, q_norm_g, w_uq, kv_norm_g, w_ukv):
    half = MLA_ROPE // 2
    z = lambda *shape: jnp.zeros(shape, F32)
    kr = w_in[:, EV_KR:EV_KR + MLA_ROPE]
    kr1, kr2 = kr[:, :half], kr[:, half:]
    tail = HEAD_PAD - MLA_NOPE - MLA_ROPE - half
    kr_blk = jnp.concatenate([z(D_MODEL, MLA_NOPE), kr1, kr2, kr1, z(D_MODEL, tail)], axis=1)
    w_in_pad = jnp.concatenate([w_in[:, :EV_KR], kr_blk], axis=1)

    wq = w_uq.reshape(MLA_Q_RANK, MLA_HEADS, MLA_NOPE + MLA_ROPE)
    nope, r1, r2 = wq[..., :MLA_NOPE], wq[..., MLA_NOPE:MLA_NOPE + half], wq[..., MLA_NOPE + half:]
    zq = z(MLA_Q_RANK, MLA_HEADS, tail)
    w_q = jnp.concatenate([nope, r1, r2, r1, zq], axis=-1).reshape(MLA_Q_RANK, MLA_PAD)

    wkv = w_ukv.reshape(MLA_KV_RANK, MLA_HEADS, MLA_NOPE + MLA_V)
    k_nope, v = wkv[..., :MLA_NOPE], wkv[..., MLA_NOPE:]
    w_k = jnp.concatenate([k_nope, z(MLA_KV_RANK, MLA_HEADS, HEAD_PAD - MLA_NOPE)], axis=-1)
    zv = jnp.zeros_like(v)
    odd_head = (jnp.arange(MLA_HEADS) % 2 == 1)[None, :, None]
    w_v = jnp.concatenate([jnp.where(odd_head, zv, v), jnp.where(odd_head, v, zv)], axis=-1)
    return {
        "w_in": w_in_pad.astype(BF16), "conv_w": conv_w, "q_norm_g": q_norm_g[None, :],
        "kv_norm_g": kv_norm_g[None, :], "w_q": w_q.astype(BF16),
        "w_k": w_k.reshape(MLA_KV_RANK, MLA_PAD).astype(BF16),
        "w_v": w_v.reshape(MLA_KV_RANK, MLA_PAD).T.astype(BF16),
    }


def _block_diag(w):
    n, r, c = w.shape
    eye = jnp.eye(n, dtype=w.dtype)
    return (eye[:, None, :, None] * w[:, :, None, :]).reshape(n * r, n * c)


def _odd_weights(w_in, gate_up, gate_bias, gla_norm_g, conv_w, conv_b, w_a, b_a, w_i, b_i, lam):
    dk, dv = GLA_HEADS * GLA_DK, GLA_HEADS * GLA_DV
    o = 0
    parts = {}
    for name, width in (("q", dk), ("k", dk), ("v", dv), ("g", GLA_GATE_RANK), ("og", dv), ("x", LRU_WIDTH),
                        ("y", LRU_WIDTH)):
        parts[name] = w_in[:, o:o + width]
        o += width
    g_pad = jnp.zeros((D_MODEL, LANES - GLA_GATE_RANK), F32)
    w_in_pad = jnp.concatenate([parts["x"], parts["y"], parts["v"], parts["og"], parts["q"], parts["k"], parts["g"],
                                g_pad], axis=1)
    gate_up_pad = jnp.concatenate([gate_up, jnp.zeros((LANES - GLA_GATE_RANK, dk), F32)], axis=0)
    return {
        "w_in": w_in_pad.astype(BF16), "gate_up": gate_up_pad.astype(BF16), "gate_bias": gate_bias[None, :],
        "gla_norm_g": gla_norm_g[None, :], "conv_w": conv_w, "conv_b": conv_b[None, :],
        "w_a": _block_diag(w_a).astype(BF16), "b_a": b_a[None, :], "w_i": _block_diag(w_i).astype(BF16),
        "b_i": b_i[None, :], "lam": lam[None, :],
    }


def kernel(x, positions, mixer_norm_g, mlp_norm_g, mlp_w1, mlp_w2, final_norm_g, ev_w_in, ev_conv_w, mla_q_norm_g, mla_w_uq, mla_kv_norm_g, mla_w_ukv, ev_w_out, od_w_in, gla_w_gate_up, gla_b_gate, gla_norm_g, lru_conv_w, lru_conv_b, lru_w_a, lru_b_a, lru_w_i, lru_b_i, lru_lambda, od_w_out):
    batch, seq, d_model = x.shape
    depth = mixer_norm_g.shape[0]
    assert d_model == D_MODEL and seq % ROW_BLOCK == 0 and ROW_BLOCK % GLA_BLOCK == 0
    assert ATTN_BLOCK == ROW_BLOCK
    h = x.reshape(batch * seq, D_MODEL)
    cos, sin = _rope_tables(positions)
    final_g = final_norm_g[None, :]
    for layer in range(depth):
        j = layer // 2
        norm_g = mixer_norm_g[layer][None, :]
        if layer % 2 == 0:
            w = _even_weights(ev_w_in[j], ev_conv_w[j], mla_q_norm_g[j], mla_w_uq[j], mla_kv_norm_g[j], mla_w_ukv[j])
            mix_a, q, k, v = _even_pre(h, norm_g, w, cos, sin, batch, seq)
            mix_b = _attention(q, k, v, batch, seq)
            w_out = ev_w_out[j]
        else:
            w = _odd_weights(od_w_in[j], gla_w_gate_up[j], gla_b_gate[j], gla_norm_g[j], lru_conv_w[j],
                             lru_conv_b[j], lru_w_a[j], lru_b_a[j], lru_w_i[j], lru_b_i[j], lru_lambda[j])
            mix_a, mix_b = _odd_pre(h, norm_g, w, batch, seq)
            w_out = od_w_out[j]
        h = _post(h, mix_a, mix_b, w_out.astype(BF16), mlp_norm_g[layer][None, :], mlp_w1[layer].astype(BF16),
                  mlp_w2[layer].astype(BF16), final_g, final_norm=(layer == depth - 1))
    return h.reshape(batch, seq, D_MODEL)
```

```python
import functools

import jax
import jax.numpy as jnp
from jax import lax
from jax.experimental import pallas as pl
from jax.experimental.pallas import tpu as pltpu

F32 = jnp.float32
BF16 = jnp.bfloat16

D_MODEL = 1024
D_FF = 4 * D_MODEL
NORM_EPS = 1e-6

SC_WIDTH = 512
SC_KERNEL = 3
MLA_HEADS = 8
MLA_NOPE = 64
MLA_ROPE = 32
MLA_V = 64
MLA_Q_RANK = 384
MLA_KV_RANK = 256
ROPE_THETA = 10000.0
QK_SCALE_LOG2E = (MLA_NOPE + MLA_ROPE) ** -0.5 * 1.4426950408889634

GLA_HEADS = 4
GLA_DK = 64
GLA_DV = 128
GLA_GATE_RANK = 16
GLA_TAU = 16.0
GLA_CHUNK = 64
LRU_WIDTH = 512
LRU_BLOCKS = 8
LRU_KERNEL = 4
LRU_C = 8.0

LANES = 128
SUBLANES = 8
VMEM_LIMIT_BYTES = 56 * 1024 * 1024

HEAD_PAD = LANES
ROPE_SWAP_ROLL = LANES - MLA_ROPE // 2
MLA_PAD = MLA_HEADS * HEAD_PAD

EV_GB, EV_GC, EV_U = 0, 512, 1024
EV_CQ = 1536
EV_CKV = EV_CQ + MLA_Q_RANK
EV_KR = EV_CKV + MLA_KV_RANK
EV_IN_PAD = EV_KR + LANES

OD_X, OD_Y, OD_V, OD_OG = 0, 512, 1024, 1536
OD_Q, OD_K, OD_G = 2048, 2304, 2560
OD_IN_PAD = OD_G + LANES

ROW_BLOCK = 512
FF_BLOCK = 1024
POST_ROW_BLOCK = 1024
POST_ROW_SPLIT = 4
ATTN_BLOCK = 512
ATTN_HEAD_GROUP = 8
ATTN_LOOKAHEAD = 2
GLA_BLOCK = 256
LRU_CHAIN_STAGE = 8
ODD_PROJ_CHUNK = 512
EVEN_PROJ_CHUNK = 256
LRU_COL_SLAB = 512


def _rms(x):
    return x * lax.rsqrt(jnp.mean(x * x, axis=-1, keepdims=True) + NORM_EPS)


def _dot(a, b):
    return jnp.dot(a, b, preferred_element_type=F32)


def _dot_nt(a, b):
    return lax.dot_general(a, b, (((1,), (1,)), ((), ())), preferred_element_type=F32)


def _dot_tn(a, b):
    return lax.dot_general(a, b, (((0,), (0,)), ((), ())), preferred_element_type=F32)


def _sigmoid(x):
    return 0.5 * jnp.tanh(0.5 * x) + 0.5


def _softplus(x):
    return jnp.maximum(x, 0.0) + jnp.log1p(jnp.exp(-jnp.abs(x)))


def _params(*semantics, flags=None):
    return pltpu.CompilerParams(dimension_semantics=semantics, vmem_limit_bytes=VMEM_LIMIT_BYTES, flags=flags)


def _causal_conv(tail_ref, x, w_ref, first_block):
    rows, width = x.shape
    taps = w_ref.shape[0]
    groups = rows // SUBLANES

    if first_block is not None:
        @pl.when(first_block)
        def _():
            tail_ref[...] = jnp.zeros((SUBLANES, width), F32)

    x3 = x.reshape(groups, SUBLANES, width)
    with_tail = jnp.concatenate([tail_ref[...][None], x3], axis=0)
    tail_ref[...] = x3[groups - 1]
    sub = lax.broadcasted_iota(jnp.int32, (groups, SUBLANES, width), 1)
    out = w_ref[taps - 1:taps, :] * x3
    for k in range(taps - 1):
        shift = taps - 1 - k
        rolled = pltpu.roll(with_tail, shift, axis=1)
        shifted = jnp.where(sub >= shift, rolled[1:], rolled[:-1])
        out = out + w_ref[k:k + 1, :] * shifted
    return out.reshape(rows, width)


def _rope_table_kernel(pos_ref, freq_ref, sign_ref, keep_ref, cos_ref, sin_ref):
    ang = pos_ref[...].astype(F32) * freq_ref[...]
    cos_ref[...] = jnp.cos(ang) * keep_ref[...]
    sin_ref[...] = jnp.sin(ang) * sign_ref[...]


def _rope_tables(positions):
    b, s = positions.shape
    t = b * s
    inv_freq = ROPE_THETA ** (-jnp.arange(0, MLA_ROPE, 2, dtype=F32) / MLA_ROPE)
    half = MLA_ROPE // 2
    zeros_lo = jnp.zeros((MLA_NOPE,), F32)
    zeros_hi = jnp.zeros((HEAD_PAD - MLA_NOPE - MLA_ROPE,), F32)
    freq = jnp.concatenate([zeros_lo, inv_freq, inv_freq, zeros_hi])[None, :]
    sign = jnp.concatenate([zeros_lo, -jnp.ones((half,), F32), jnp.ones((half,), F32), zeros_hi])[None, :]
    keep = jnp.concatenate([jnp.ones((MLA_NOPE + MLA_ROPE,), F32), zeros_hi])[None, :]
    ts = ROW_BLOCK
    row = pl.BlockSpec((ts, 1), lambda i: (i, 0))
    lane = pl.BlockSpec((1, HEAD_PAD), lambda i: (0, 0))
    out = pl.BlockSpec((ts, HEAD_PAD), lambda i: (i, 0))
    return pl.pallas_call(
        _rope_table_kernel,
        grid=(t // ts,),
        in_specs=[row, lane, lane, lane],
        out_specs=[out, out],
        out_shape=[jax.ShapeDtypeStruct((t, HEAD_PAD), F32)] * 2,
        compiler_params=_params("parallel"),
        name="rope_tables",
    )(positions.reshape(t, 1), freq, sign, keep)


def _even_pre_kernel(h_ref, g_ref, win_ref, cw_ref, qg_ref, kvg_ref, wq_ref, wk_ref, wv_ref, cos_ref, sin_ref,
                     sc_ref, q_ref, k_ref, v_ref, tail_ref, proj_even_ref, proj_odd_ref, *, blocks_per_seq):
    j = pl.program_id(0)

    @pl.when(j == 0)
    def _():
        proj_odd_ref[...] = jnp.zeros(proj_odd_ref.shape, F32)

    @pl.when((j == 0) | ((j - 1) % blocks_per_seq == 0))
    def _():
        tail_ref[...] = jnp.zeros(tail_ref.shape, F32)

    def step(write_ref, read_ref):
        hn = (_rms(h_ref[...]) * g_ref[...]).astype(BF16)

        def project():
            for start in range(0, EV_IN_PAD, EVEN_PROJ_CHUNK):
                cols = slice(start, min(start + EVEN_PROJ_CHUNK, EV_IN_PAD))
                write_ref[:, cols] = _dot(hn, win_ref[:, cols])
                yield

        def finish():
            gate_b = read_ref[:, EV_GB:EV_GB + SC_WIDTH]
            cu = read_ref[:, EV_GC:EV_GC + SC_WIDTH] * read_ref[:, EV_U:EV_U + SC_WIDTH]
            sc_ref[...] = (gate_b * _causal_conv(tail_ref, cu, cw_ref, None)).astype(BF16)
            yield
            cqn = (_rms(read_ref[:, EV_CQ:EV_CQ + MLA_Q_RANK]) * qg_ref[...]).astype(BF16)
            q0 = _dot(cqn, wq_ref[...])
            yield
            ckvn = (_rms(read_ref[:, EV_CKV:EV_CKV + MLA_KV_RANK]) * kvg_ref[...]).astype(BF16)
            kn = _dot(ckvn, wk_ref[...])
            yield
            vt = _dot_nt(wv_ref[...], ckvn)
            feat = lax.broadcasted_iota(jnp.int32, vt.shape, 0)
            ones_feat = (1 - (feat // HEAD_PAD) % 2) * MLA_V
            v_ref[...] = jnp.where(feat % HEAD_PAD == ones_feat, 1.0, vt).astype(BF16)
            yield
            cos = cos_ref[...]
            sin = sin_ref[...]
            rope = lambda x: x * cos + pltpu.roll(x, ROPE_SWAP_ROLL, axis=1) * sin
            k_rope = rope(read_ref[:, EV_KR:EV_KR + LANES])
            for head in range(MLA_HEADS):
                sl = slice(head * HEAD_PAD, (head + 1) * HEAD_PAD)
                q_ref[:, sl] = (rope(q0[:, sl]) * QK_SCALE_LOG2E).astype(BF16)
                k_ref[:, sl] = (kn[:, sl] + k_rope).astype(BF16)
                yield

        _interleave(project(), finish())

    @pl.when(j % 2 == 0)
    def _():
        step(proj_even_ref, proj_odd_ref)

    @pl.when(j % 2 == 1)
    def _():
        step(proj_odd_ref, proj_even_ref)


def _even_pre(h, norm_g, w, cos, sin, batch, seq):
    t = batch * seq
    tm = ROW_BLOCK
    nblk = seq // tm
    n_steps = t // tm + 1
    full = lambda arr: pl.BlockSpec(arr.shape, lambda j: (0,) * arr.ndim)
    consts = [norm_g, w["w_in"], w["conv_w"], w["q_norm_g"], w["kv_norm_g"], w["w_q"], w["w_k"], w["w_v"]]
    prev = lambda j: jnp.maximum(j - 1, 0)
    block_in = pl.BlockSpec((tm, D_MODEL), lambda j: (jnp.minimum(j, n_steps - 2), 0))
    rows_prev = lambda width: pl.BlockSpec((tm, width), lambda j: (prev(j), 0))
    return pl.pallas_call(
        functools.partial(_even_pre_kernel, blocks_per_seq=nblk),
        grid=(n_steps,),
        in_specs=[block_in] + [full(a) for a in consts] + [rows_prev(HEAD_PAD), rows_prev(HEAD_PAD)],
        out_specs=[rows_prev(SC_WIDTH), rows_prev(MLA_PAD), rows_prev(MLA_PAD),
                   pl.BlockSpec((None, None, MLA_PAD, tm), lambda j: (prev(j) // nblk, prev(j) % nblk, 0, 0))],
        out_shape=[jax.ShapeDtypeStruct((t, SC_WIDTH), BF16)] + [jax.ShapeDtypeStruct((t, MLA_PAD), BF16)] * 2
        + [jax.ShapeDtypeStruct((batch, nblk, MLA_PAD, tm), BF16)],
        scratch_shapes=[pltpu.VMEM((SUBLANES, SC_WIDTH), F32),
                        pltpu.VMEM((tm, EV_IN_PAD), F32),
                        pltpu.VMEM((tm, EV_IN_PAD), F32)],
        compiler_params=_params("arbitrary"),
        name="even_pre",
    )(h, *consts, cos, sin)


def _attn_kernel(q_ref, k_ref, v_ref, o_ref, acc_ref):
    tq = q_ref.shape[0]
    qi = pl.program_id(1)
    key = lax.broadcasted_iota(jnp.int32, (tq, tq), 0)
    query = lax.broadcasted_iota(jnp.int32, (tq, tq), 1)
    causal = key <= query
    feat = lax.broadcasted_iota(jnp.int32, (HEAD_PAD, tq), 0)

    def scores(head, j, masked):
        sl = slice(head * HEAD_PAD, (head + 1) * HEAD_PAD)
        start = pl.multiple_of(j * tq, tq)
        s = _dot_nt(k_ref[pl.ds(start, tq), sl], q_ref[:, sl])
        return jnp.where(causal, s, -jnp.inf) if masked else s

    def accumulate(head, j, m, s):
        sl = slice(head * HEAD_PAD, (head + 1) * HEAD_PAD)
        m_new = jnp.maximum(m, jnp.max(s, axis=0, keepdims=True))
        p = jnp.exp2(s - m_new)
        alpha = jnp.exp2(m - m_new)
        acc_ref[head] = alpha * acc_ref[head] + _dot(v_ref[j, sl, :], p.astype(BF16))
        return m_new

    def run(heads, tiles, ms):
        ms = list(ms)
        work = [(j, masked, idx) for j, masked in tiles for idx in range(len(heads))]
        pending = [scores(heads[idx], j, masked) for j, masked, idx in work[:ATTN_LOOKAHEAD]]
        for n, (j, masked, idx) in enumerate(work):
            if n + ATTN_LOOKAHEAD < len(work):
                j2, masked2, idx2 = work[n + ATTN_LOOKAHEAD]
                pending.append(scores(heads[idx2], j2, masked2))
            ms[idx] = accumulate(heads[idx], j, ms[idx], pending.pop(0))
        return tuple(ms)

    for group in range(MLA_HEADS // ATTN_HEAD_GROUP):
        heads = tuple(range(group * ATTN_HEAD_GROUP, (group + 1) * ATTN_HEAD_GROUP))
        for h in heads:
            acc_ref[h] = jnp.zeros((HEAD_PAD, tq), F32)
        m0 = jnp.full((1, tq), -jnp.inf, F32)
        ms = lax.fori_loop(0, qi // 2, lambda jj, ms: run(heads, [(2 * jj, False), (2 * jj + 1, False)], ms),
                           (m0,) * ATTN_HEAD_GROUP)

        @pl.when(qi % 2 == 1)
        def _():
            run(heads, [(qi - 1, False), (qi, True)], ms)

        @pl.when(qi % 2 == 0)
        def _():
            run(heads, [(qi, True)], ms)

        for even in heads[::2]:
            acc_even = acc_ref[even]
            acc_odd = acc_ref[even + 1]
            out = jnp.where(feat < MLA_V, acc_even / acc_even[MLA_V:MLA_V + 1, :], acc_odd / acc_odd[0:1, :])
            o_ref[:, (even // 2) * LANES:(even // 2 + 1) * LANES] = out.T.astype(BF16)


def _attention(q, k, vt, batch, seq):
    tq = ATTN_BLOCK
    q3, k3 = (a.reshape(batch, seq, MLA_PAD) for a in (q, k))
    out = pl.pallas_call(
        _attn_kernel,
        grid=(batch, seq // tq),
        in_specs=[pl.BlockSpec((None, tq, MLA_PAD), lambda b, i: (b, i, 0)),
                  pl.BlockSpec((None, seq, MLA_PAD), lambda b, i: (b, 0, 0)),
                  pl.BlockSpec((None,) + vt.shape[1:], lambda b, i: (b, 0, 0, 0))],
        out_specs=pl.BlockSpec((None, tq, MLA_HEADS * MLA_V), lambda b, i: (b, i, 0)),
        out_shape=jax.ShapeDtypeStruct((batch, seq, MLA_HEADS * MLA_V), BF16),
        scratch_shapes=[pltpu.VMEM((MLA_HEADS, HEAD_PAD, tq), F32)],
        compiler_params=_params("parallel", "arbitrary"),
        name="mla_attention",
    )(q3, k3, vt)
    return out.reshape(batch * seq, MLA_HEADS * MLA_V)


def _interleave(*stage_generators):
    live = list(stage_generators)
    while live:
        for gen in list(live):
            try:
                next(gen)
            except StopIteration:
                live.remove(gen)


def _gla_stages(q, k, v, g, og, gup_ref, gbias_ref, ng_ref, states, o_ref, rows):
    tb = q.shape[0]
    n_pairs = GLA_HEADS // 2
    pair_dk = 2 * GLA_DK
    pair_dv = 2 * GLA_DV
    n_chunks = tb // GLA_CHUNK

    gate = _dot(g.astype(BF16), gup_ref[...]) + gbias_ref[...]
    log_a = (jnp.minimum(gate, 0.0) - jnp.log1p(jnp.exp(-jnp.abs(gate)))) / GLA_TAU
    yield

    row = lax.broadcasted_iota(jnp.int32, (tb, tb), 0)
    col = lax.broadcasted_iota(jnp.int32, (tb, tb), 1)
    tril = ((row // GLA_CHUNK) == (col // GLA_CHUNK)) & (col <= row)
    tril_bf = tril.astype(BF16)
    cum = None
    rest = log_a
    for _ in range(3):
        piece = rest.astype(BF16)
        rest = rest - piece.astype(F32)
        part = _dot(tril_bf, piece)
        cum = part if cum is None else cum + part
    chunk_last = [cum[(c + 1) * GLA_CHUNK - 1:(c + 1) * GLA_CHUNK, :] for c in range(n_chunks)]
    last = jnp.concatenate([jnp.broadcast_to(t, (GLA_CHUNK, t.shape[1])) for t in chunk_last], axis=0)
    yield

    q_in = (q * (GLA_DK ** -0.5)) * jnp.exp(cum)
    k_in = k * jnp.exp(-cum)
    k_out = k * jnp.exp(last - cum)
    yield

    lane = lax.broadcasted_iota(jnp.int32, (tb, pair_dk), 1)
    state_rows = lax.broadcasted_iota(jnp.int32, (pair_dv, pair_dk), 0)
    state_lanes = lax.broadcasted_iota(jnp.int32, (pair_dv, pair_dk), 1)
    own_block = (state_rows < GLA_DV) == (state_lanes < GLA_DK)

    ksl = [slice(p * pair_dk, (p + 1) * pair_dk) for p in range(n_pairs)]
    vsl = [slice(p * pair_dv, (p + 1) * pair_dv) for p in range(n_pairs)]

    kvs = [[None] * n_chunks for _ in range(n_pairs)]
    for p in range(n_pairs):
        for c in range(n_chunks):
            crow = slice(c * GLA_CHUNK, (c + 1) * GLA_CHUNK)
            kvs[p][c] = jnp.where(own_block, _dot_tn(v[crow, vsl[p]], k_out[crow, ksl[p]].astype(BF16)), 0.0)
        yield
    entering = [[] for _ in range(n_pairs)]
    for c in range(n_chunks):
        for p in range(n_pairs):
            entering[p].append(states[p])
            states[p] = states[p] * jnp.exp(chunk_last[c][:, ksl[p]]) + kvs[p][c]
    yield

    for p in range(n_pairs):
        qp, kp = q_in[:, ksl[p]], k_in[:, ksl[p]]
        o_inter = jnp.concatenate(
            [_dot_nt(qp[c * GLA_CHUNK:(c + 1) * GLA_CHUNK].astype(BF16), entering[p][c].astype(BF16))
             for c in range(n_chunks)], axis=0)
        yield
        for sub in range(2):
            mine = (lane < GLA_DK) if sub == 0 else (lane >= GLA_DK)
            qh = jnp.where(mine, qp, 0.0).astype(BF16)
            kh = jnp.where(mine, kp, 0.0).astype(BF16)
            att = jnp.where(tril, _dot_nt(qh, kh), 0.0)
            head = 2 * p + sub
            hsl = slice(head * GLA_DV, (head + 1) * GLA_DV)
            o = _dot(att.astype(BF16), v[:, hsl]) + o_inter[:, sub * GLA_DV:(sub + 1) * GLA_DV]
            o = _rms(o) * ng_ref[...]
            gate_o = og[:, hsl]
            o_ref[rows, hsl] = (o * (gate_o * _sigmoid(gate_o))).astype(BF16)
            yield


def _lru_stages(x, y, cw_ref, cb_ref, wa_ref, ba_ref, wi_ref, bi_ref, lam_ref, tail_ref, carry_ref, o_ref,
                first_block):
    tb = x.shape[0]
    xc = _causal_conv(tail_ref, x, cw_ref, first_block) + cb_ref[...]
    xcb = xc.astype(BF16)
    yield

    groups = tb // SUBLANES
    sub = lax.broadcasted_iota(jnp.int32, (groups, SUBLANES, LRU_COL_SLAB), 1)
    for start in range(0, LRU_WIDTH, LRU_COL_SLAB):
        cols = slice(start, start + LRU_COL_SLAB)
        r = _sigmoid(_dot(xcb, wa_ref[:, cols]) + ba_ref[:, cols])
        yield
        gate_i = _sigmoid(_dot(xcb, wi_ref[:, cols]) + bi_ref[:, cols])
        yield
        log_a = (-LRU_C * r) * _softplus(-lam_ref[:, cols])
        a = jnp.exp(log_a)
        b = jnp.sqrt(-jnp.tanh(log_a) * (a * a + 1.0)) * (gate_i * xc[:, cols])
        yield

        a = a.reshape(groups, SUBLANES, LRU_COL_SLAB)
        b = b.reshape(groups, SUBLANES, LRU_COL_SLAB)
        shift = 1
        while shift < SUBLANES:
            a_prev = pltpu.roll(a, shift, axis=1)
            b_prev = pltpu.roll(b, shift, axis=1)
            take = sub >= shift
            b = jnp.where(take, a * b_prev + b, b)
            a = jnp.where(take, a * a_prev, a)
            shift *= 2
            yield

        h_prev = carry_ref[:, cols]
        hs = []
        for grp in range(groups):
            h = a[grp] * h_prev + b[grp]
            hs.append(h)
            h_prev = jnp.broadcast_to(h[SUBLANES - 1:SUBLANES, :], (SUBLANES, LRU_COL_SLAB))
            if grp % LRU_CHAIN_STAGE == LRU_CHAIN_STAGE - 1:
                yield
        carry_ref[:, cols] = h_prev
        hseq = jnp.stack(hs, axis=0).reshape(tb, LRU_COL_SLAB)
        ys = y[:, cols]
        gelu = ys * (0.5 * (1.0 + jnp.tanh(0.7978845608028654 * (ys + 0.044715 * (ys * ys * ys)))))
        o_ref[:, cols] = (gelu * hseq).astype(BF16)
        yield


def _odd_pre_kernel(h_ref, g_ref, win_ref, gup_ref, gbias_ref, ng_ref, cw_ref, cb_ref, wa_ref, ba_ref, wi_ref,
                    bi_ref, lam_ref, gla_ref, lru_ref, state_ref, tail_ref, carry_ref, proj_even_ref, proj_odd_ref,
                    *, blocks_per_seq):
    j = pl.program_id(0)
    width_k = GLA_HEADS * GLA_DK
    width_v = GLA_HEADS * GLA_DV

    @pl.when(j == 0)
    def _():
        proj_odd_ref[...] = jnp.zeros(proj_odd_ref.shape, F32)

    @pl.when((j == 0) | ((j - 1) % blocks_per_seq == 0))
    def _():
        state_ref[...] = jnp.zeros(state_ref.shape, F32)
        carry_ref[...] = jnp.zeros(carry_ref.shape, F32)
        tail_ref[...] = jnp.zeros(tail_ref.shape, F32)

    def step(write_ref, read_ref):
        hn = (_rms(h_ref[...]) * g_ref[...]).astype(BF16)

        def project():
            for start in range(0, OD_IN_PAD, ODD_PROJ_CHUNK):
                cols = slice(start, min(start + ODD_PROJ_CHUNK, OD_IN_PAD))
                write_ref[:, cols] = _dot(hn, win_ref[:, cols])
                yield

        states = [state_ref[p] for p in range(GLA_HEADS // 2)]

        gla_blocks = []
        for blk in range(h_ref.shape[0] // GLA_BLOCK):
            rows = slice(blk * GLA_BLOCK, (blk + 1) * GLA_BLOCK)
            gla_blocks.append(_gla_stages(
                read_ref[rows, OD_Q:OD_Q + width_k], read_ref[rows, OD_K:OD_K + width_k],
                read_ref[rows, OD_V:OD_V + width_v].astype(BF16), read_ref[rows, OD_G:OD_G + LANES],
                read_ref[rows, OD_OG:OD_OG + width_v], gup_ref, gbias_ref, ng_ref, states, gla_ref, rows))
        lru = _lru_stages(read_ref[:, OD_X:OD_X + LRU_WIDTH], read_ref[:, OD_Y:OD_Y + LRU_WIDTH], cw_ref, cb_ref,
                          wa_ref, ba_ref, wi_ref, bi_ref, lam_ref, tail_ref, carry_ref, lru_ref, None)
        _interleave(project(), lru, *gla_blocks)
        for p, s in enumerate(states):
            state_ref[p] = s

    @pl.when(j % 2 == 0)
    def _():
        step(proj_even_ref, proj_odd_ref)

    @pl.when(j % 2 == 1)
    def _():
        step(proj_odd_ref, proj_even_ref)


def _odd_pre(h, norm_g, w, batch, seq):
    t = batch * seq
    tm = ROW_BLOCK
    nblk = seq // tm
    n_steps = t // tm + 1
    full = lambda arr: pl.BlockSpec(arr.shape, lambda j: (0,) * arr.ndim)
    consts = [norm_g, w["w_in"], w["gate_up"], w["gate_bias"], w["gla_norm_g"], w["conv_w"], w["conv_b"], w["w_a"],
              w["b_a"], w["w_i"], w["b_i"], w["lam"]]
    width_v = GLA_HEADS * GLA_DV
    block_in = pl.BlockSpec((tm, D_MODEL), lambda j: (jnp.minimum(j, n_steps - 2), 0))
    block_out = lambda width: pl.BlockSpec((tm, width), lambda j: (jnp.maximum(j - 1, 0), 0))
    return pl.pallas_call(
        functools.partial(_odd_pre_kernel, blocks_per_seq=nblk),
        grid=(n_steps,),
        in_specs=[block_in] + [full(a) for a in consts],
        out_specs=[block_out(width_v), block_out(LRU_WIDTH)],
        out_shape=[jax.ShapeDtypeStruct((t, width_v), BF16), jax.ShapeDtypeStruct((t, LRU_WIDTH), BF16)],
        scratch_shapes=[pltpu.VMEM((GLA_HEADS // 2, 2 * GLA_DV, 2 * GLA_DK), F32),
                        pltpu.VMEM((SUBLANES, LRU_WIDTH), F32),
                        pltpu.VMEM((SUBLANES, LRU_WIDTH), F32),
                        pltpu.VMEM((tm, OD_IN_PAD), F32),
                        pltpu.VMEM((tm, OD_IN_PAD), F32)],
        compiler_params=_params("arbitrary"),
        name="odd_pre",
    )(h, *consts)


def _post_kernel(h_ref, a_ref, b_ref, wo_ref, g_ref, w1_ref, w2_ref, gf_ref, o_ref, *, final_norm):
    half = a_ref.shape[1]
    sub = h_ref.shape[0] // POST_ROW_SPLIT
    for r in range(POST_ROW_SPLIT):
        rows = slice(r * sub, (r + 1) * sub)
        mix = _dot(a_ref[rows, :], wo_ref[0:half, :]) + _dot(b_ref[rows, :], wo_ref[half:2 * half, :])
        h1 = h_ref[rows, :] + mix
        hn = (_rms(h1) * g_ref[...]).astype(BF16)
        out = h1
        for c in range(D_FF // FF_BLOCK):
            cols = slice(c * FF_BLOCK, (c + 1) * FF_BLOCK)
            hidden = _dot(hn, w1_ref[:, cols])
            act = jnp.square(jnp.maximum(hidden, 0.0)).astype(BF16)
            out = out + _dot(act, w2_ref[cols, :])
        if final_norm:
            out = _rms(out) * gf_ref[...]
        o_ref[rows, :] = out


def _post(h, a, b, w_out, out_layer, norm_g, w1, w2, mlp_layer, final_g, final_norm):
    t = h.shape[0]
    tm = POST_ROW_BLOCK
    rows = lambda width: pl.BlockSpec((tm, width), lambda i: (i, 0))
    vec = pl.BlockSpec((1, D_MODEL), lambda i: (0, 0))
    resident = lambda arr, layer: pl.BlockSpec((None,) + arr.shape[1:], lambda i: (layer, 0, 0),
                                               pipeline_mode=pl.Buffered(1))
    return pl.pallas_call(
        functools.partial(_post_kernel, final_norm=final_norm),
        grid=(t // tm,),
        in_specs=[rows(D_MODEL), rows(a.shape[1]), rows(b.shape[1]), resident(w_out, out_layer), vec,
                  resident(w1, mlp_layer), resident(w2, mlp_layer), vec],
        out_specs=rows(D_MODEL),
        out_shape=jax.ShapeDtypeStruct((t, D_MODEL), F32),
        compiler_params=_params("parallel"),
        name="post_mlp",
    )(h, a, b, w_out, norm_g, w1, w2, final_g)


def _even_weights(w_in, conv_w, q_norm_g, w_uq, kv_norm_g, w_ukv):
    half = MLA_ROPE // 2
    z = lambda *shape: jnp.zeros(shape, F32)
    kr = w_in[:, EV_KR:EV_KR + MLA_ROPE]
    kr1, kr2 = kr[:, :half], kr[:, half:]
    tail = HEAD_PAD - MLA_NOPE - MLA_ROPE - half
    kr_blk = jnp.concatenate([z(D_MODEL, MLA_NOPE), kr1, kr2, kr1, z(D_MODEL, tail)], axis=1)
    w_in_pad = jnp.concatenate([w_in[:, :EV_KR], kr_blk], axis=1)

    wq = w_uq.reshape(MLA_Q_RANK, MLA_HEADS, MLA_NOPE + MLA_ROPE)
    nope, r1, r2 = wq[..., :MLA_NOPE], wq[..., MLA_NOPE:MLA_NOPE + half], wq[..., MLA_NOPE + half:]
    zq = z(MLA_Q_RANK, MLA_HEADS, tail)
    w_q = jnp.concatenate([nope, r1, r2, r1, zq], axis=-1).reshape(MLA_Q_RANK, MLA_PAD)

    wkv = w_ukv.reshape(MLA_KV_RANK, MLA_HEADS, MLA_NOPE + MLA_V)
    k_nope, v = wkv[..., :MLA_NOPE], wkv[..., MLA_NOPE:]
    w_k = jnp.concatenate([k_nope, z(MLA_KV_RANK, MLA_HEADS, HEAD_PAD - MLA_NOPE)], axis=-1)
    zv = jnp.zeros_like(v)
    odd_head = (jnp.arange(MLA_HEADS) % 2 == 1)[None, :, None]
    w_v = jnp.concatenate([jnp.where(odd_head, zv, v), jnp.where(odd_head, v, zv)], axis=-1)
    return {
        "w_in": w_in_pad.astype(BF16), "conv_w": conv_w, "q_norm_g": q_norm_g[None, :],
        "kv_norm_g": kv_norm_g[None, :], "w_q": w_q.astype(BF16),
        "w_k": w_k.reshape(MLA_KV_RANK, MLA_PAD).astype(BF16),
        "w_v": w_v.reshape(MLA_KV_RANK, MLA_PAD).T.astype(BF16),
    }


def _block_diag(w):
    n, r, c = w.shape
    eye = jnp.eye(n, dtype=w.dtype)
    return (eye[:, None, :, None] * w[:, :, None, :]).reshape(n * r, n * c)


def _odd_weights(w_in, gate_up, gate_bias, gla_norm_g, conv_w, conv_b, w_a, b_a, w_i, b_i, lam):
    dk, dv = GLA_HEADS * GLA_DK, GLA_HEADS * GLA_DV
    o = 0
    parts = {}
    for name, width in (("q", dk), ("k", dk), ("v", dv), ("g", GLA_GATE_RANK), ("og", dv), ("x", LRU_WIDTH),
                        ("y", LRU_WIDTH)):
        parts[name] = w_in[:, o:o + width]
        o += width
    g_pad = jnp.zeros((D_MODEL, LANES - GLA_GATE_RANK), F32)
    w_in_pad = jnp.concatenate([parts["x"], parts["y"], parts["v"], parts["og"], parts["q"], parts["k"], parts["g"],
                                g_pad], axis=1)
    gate_up_pad = jnp.concatenate([gate_up, jnp.zeros((LANES - GLA_GATE_RANK, dk), F32)], axis=0)
    return {
        "w_in": w_in_pad.astype(BF16), "gate_up": gate_up_pad.astype(BF16), "gate_bias": gate_bias[None, :],
        "gla_norm_g": gla_norm_g[None, :], "conv_w": conv_w, "conv_b": conv_b[None, :],
        "w_a": _block_diag(w_a).astype(BF16), "b_a": b_a[None, :], "w_i": _block_diag(w_i).astype(BF16),
        "b_i": b_i[None, :], "lam": lam[None, :],
    }


def kernel(x, positions, mixer_norm_g, mlp_norm_g, mlp_w1, mlp_w2, final_norm_g, ev_w_in, ev_conv_w, mla_q_norm_g, mla_w_uq, mla_kv_norm_g, mla_w_ukv, ev_w_out, od_w_in, gla_w_gate_up, gla_b_gate, gla_norm_g, lru_conv_w, lru_conv_b, lru_w_a, lru_b_a, lru_w_i, lru_b_i, lru_lambda, od_w_out):
    batch, seq, d_model = x.shape
    depth = mixer_norm_g.shape[0]
    assert d_model == D_MODEL and seq % ROW_BLOCK == 0 and ROW_BLOCK % GLA_BLOCK == 0
    assert ATTN_BLOCK == ROW_BLOCK
    h = x.reshape(batch * seq, D_MODEL)
    cos, sin = _rope_tables(positions)
    final_g = final_norm_g[None, :]
    w1_bf, w2_bf = mlp_w1.astype(BF16), mlp_w2.astype(BF16)
    ev_w_out_bf, od_w_out_bf = ev_w_out.astype(BF16), od_w_out.astype(BF16)
    for layer in range(depth):
        j = layer // 2
        norm_g = mixer_norm_g[layer][None, :]
        if layer % 2 == 0:
            w = _even_weights(ev_w_in[j], ev_conv_w[j], mla_q_norm_g[j], mla_w_uq[j], mla_kv_norm_g[j], mla_w_ukv[j])
            mix_a, q, k, v = _even_pre(h, norm_g, w, cos, sin, batch, seq)
            mix_b = _attention(q, k, v, batch, seq)
            w_out = ev_w_out_bf
        else:
            w = _odd_weights(od_w_in[j], gla_w_gate_up[j], gla_b_gate[j], gla_norm_g[j], lru_conv_w[j],
                             lru_conv_b[j], lru_w_a[j], lru_b_a[j], lru_w_i[j], lru_b_i[j], lru_lambda[j])
            mix_a, mix_b = _odd_pre(h, norm_g, w, batch, seq)
            w_out = od_w_out_bf
        h = _post(h, mix_a, mix_b, w_out, j, mlp_norm_g[layer][None, :], w1_bf, w2_bf, layer, final_g,
                  final_norm=(layer == depth - 1))
    return h.reshape(batch, seq, D_MODEL)
```

```python
import functools

import jax
import jax.numpy as jnp
from jax import lax
from jax.experimental import pallas as pl
from jax.experimental.pallas import tpu as pltpu

F32 = jnp.float32
BF16 = jnp.bfloat16

D_MODEL = 1024
D_FF = 4 * D_MODEL
NORM_EPS = 1e-6

SC_WIDTH = 512
SC_KERNEL = 3
MLA_HEADS = 8
MLA_NOPE = 64
MLA_ROPE = 32
MLA_V = 64
MLA_Q_RANK = 384
MLA_KV_RANK = 256
ROPE_THETA = 10000.0
QK_SCALE_LOG2E = (MLA_NOPE + MLA_ROPE) ** -0.5 * 1.4426950408889634

GLA_HEADS = 4
GLA_DK = 64
GLA_DV = 128
GLA_GATE_RANK = 16
GLA_TAU = 16.0
GLA_CHUNK = 64
LRU_WIDTH = 512
LRU_BLOCKS = 8
LRU_KERNEL = 4
LRU_C = 8.0

LANES = 128
SUBLANES = 8
VMEM_LIMIT_BYTES = 56 * 1024 * 1024

HEAD_PAD = LANES
ROPE_SWAP_ROLL = LANES - MLA_ROPE // 2
ROPE_GROUPS = LANES // MLA_ROPE
ROPE_TABLE_BLOCK = 1024
MLA_PAD = MLA_HEADS * HEAD_PAD

EV_GB, EV_GC, EV_U = 0, 512, 1024
EV_CQ = 1536
EV_CKV = EV_CQ + MLA_Q_RANK
EV_KR = EV_CKV + MLA_KV_RANK
EV_IN_PAD = EV_KR + LANES

OD_X, OD_Y, OD_V, OD_OG = 0, 512, 1024, 1536
OD_Q, OD_K, OD_G = 2048, 2304, 2560
OD_IN_PAD = OD_G + LANES

ROW_BLOCK = 512
FF_BLOCK = 1024
POST_ROW_BLOCK = 1024
POST_ROW_SPLIT = 4
ATTN_BLOCK = 512
ATTN_HEAD_GROUP = 8
ATTN_LOOKAHEAD = 2
GLA_BLOCK = 256
LRU_CHAIN_STAGE = 8
ODD_PROJ_CHUNK = 512
EVEN_PROJ_CHUNK = 512
LRU_COL_SLAB = 512


def _rms(x):
    return x * lax.rsqrt(jnp.mean(x * x, axis=-1, keepdims=True) + NORM_EPS)


def _dot(a, b):
    return jnp.dot(a, b, preferred_element_type=F32)


def _dot_nt(a, b):
    return lax.dot_general(a, b, (((1,), (1,)), ((), ())), preferred_element_type=F32)


def _dot_tn(a, b):
    return lax.dot_general(a, b, (((0,), (0,)), ((), ())), preferred_element_type=F32)


def _sigmoid(x):
    return 0.5 * jnp.tanh(0.5 * x) + 0.5


def _softplus(x):
    return jnp.maximum(x, 0.0) + jnp.log1p(jnp.exp(-jnp.abs(x)))


def _params(*semantics, flags=None):
    return pltpu.CompilerParams(dimension_semantics=semantics, vmem_limit_bytes=VMEM_LIMIT_BYTES, flags=flags)


def _causal_conv(tail_ref, x, w_ref, first_block):
    rows, width = x.shape
    taps = w_ref.shape[0]
    groups = rows // SUBLANES

    if first_block is not None:
        @pl.when(first_block)
        def _():
            tail_ref[...] = jnp.zeros((SUBLANES, width), F32)

    x3 = x.reshape(groups, SUBLANES, width)
    with_tail = jnp.concatenate([tail_ref[...][None], x3], axis=0)
    tail_ref[...] = x3[groups - 1]
    sub = lax.broadcasted_iota(jnp.int32, (groups, SUBLANES, width), 1)
    out = w_ref[taps - 1:taps, :] * x3
    for k in range(taps - 1):
        shift = taps - 1 - k
        rolled = pltpu.roll(with_tail, shift, axis=1)
        shifted = jnp.where(sub >= shift, rolled[1:], rolled[:-1])
        out = out + w_ref[k:k + 1, :] * shifted
    return out.reshape(rows, width)


def _rope_table_kernel(pos_ref, freq_ref, sign_ref, cos_ref, sin_ref):
    ang = pos_ref[...].astype(F32) * freq_ref[...]
    cos_ref[...] = jnp.cos(ang)
    sin_ref[...] = jnp.sin(ang) * sign_ref[...]


def _rope_tables(positions):
    b, s = positions.shape
    t = b * s
    quarter = ROW_BLOCK // ROPE_GROUPS
    inv_freq = ROPE_THETA ** (-jnp.arange(0, MLA_ROPE, 2, dtype=F32) / MLA_ROPE)
    half = MLA_ROPE // 2
    freq = jnp.tile(jnp.concatenate([inv_freq, inv_freq]), ROPE_GROUPS)[None, :]
    sign = jnp.tile(jnp.concatenate([-jnp.ones((half,), F32), jnp.ones((half,), F32)]), ROPE_GROUPS)[None, :]
    pos = positions.reshape(t // ROW_BLOCK, ROPE_GROUPS, quarter).transpose(0, 2, 1)
    pos = jnp.repeat(pos, MLA_ROPE, axis=-1).reshape(t // ROPE_GROUPS, LANES)
    ts = min(ROPE_TABLE_BLOCK, t // ROPE_GROUPS)
    rows = pl.BlockSpec((ts, LANES), lambda i: (i, 0))
    lane = pl.BlockSpec((1, LANES), lambda i: (0, 0))
    return pl.pallas_call(
        _rope_table_kernel,
        grid=(t // ROPE_GROUPS // ts,),
        in_specs=[rows, lane, lane],
        out_specs=[rows, rows],
        out_shape=[jax.ShapeDtypeStruct((t // ROPE_GROUPS, LANES), F32)] * 2,
        compiler_params=_params("parallel"),
        name="rope_tables",
    )(pos, freq, sign)


def _rope_expand(cos_dense, sin_dense):
    lane = lax.broadcasted_iota(jnp.int32, cos_dense.shape, 1)
    rope_lanes = (lane >= MLA_NOPE) & (lane < MLA_NOPE + MLA_ROPE)
    base = jnp.where(lane < MLA_NOPE, 1.0, 0.0)
    cos, sin = [], []
    for group in range(ROPE_GROUPS):
        shift = (MLA_NOPE - MLA_ROPE * group) % LANES
        cos.append(jnp.where(rope_lanes, pltpu.roll(cos_dense, shift, axis=1), base))
        sin.append(jnp.where(rope_lanes, pltpu.roll(sin_dense, shift, axis=1), 0.0))
    return jnp.concatenate(cos, axis=0), jnp.concatenate(sin, axis=0)


def _even_pre_kernel(h_ref, g_ref, win_ref, cw_ref, qg_ref, kvg_ref, wq_ref, wk_ref, wv_ref, cos_ref, sin_ref,
                     sc_ref, q_ref, k_ref, v_ref, tail_ref, proj_even_ref, proj_odd_ref, *, blocks_per_seq):
    j = pl.program_id(0)

    @pl.when(j == 0)
    def _():
        proj_odd_ref[...] = jnp.zeros(proj_odd_ref.shape, F32)

    @pl.when((j == 0) | ((j - 1) % blocks_per_seq == 0))
    def _():
        tail_ref[...] = jnp.zeros(tail_ref.shape, F32)

    def step(write_ref, read_ref):
        hn = (_rms(h_ref[...]) * g_ref[...]).astype(BF16)

        def project():
            for start in range(0, EV_IN_PAD, EVEN_PROJ_CHUNK):
                cols = slice(start, min(start + EVEN_PROJ_CHUNK, EV_IN_PAD))
                write_ref[:, cols] = _dot(hn, win_ref[:, cols])
                yield

        def finish():
            gate_b = read_ref[:, EV_GB:EV_GB + SC_WIDTH]
            cu = read_ref[:, EV_GC:EV_GC + SC_WIDTH] * read_ref[:, EV_U:EV_U + SC_WIDTH]
            sc_ref[...] = (gate_b * _causal_conv(tail_ref, cu, cw_ref, None)).astype(BF16)
            yield
            cqn = (_rms(read_ref[:, EV_CQ:EV_CQ + MLA_Q_RANK]) * qg_ref[...]).astype(BF16)
            q0 = _dot(cqn, wq_ref[...])
            yield
            ckvn = (_rms(read_ref[:, EV_CKV:EV_CKV + MLA_KV_RANK]) * kvg_ref[...]).astype(BF16)
            kn = _dot(ckvn, wk_ref[...])
            yield
            vt = _dot_nt(wv_ref[...], ckvn)
            feat = lax.broadcasted_iota(jnp.int32, vt.shape, 0)
            ones_feat = (1 - (feat // HEAD_PAD) % 2) * MLA_V
            v_ref[...] = jnp.where(feat % HEAD_PAD == ones_feat, 1.0, vt).astype(BF16)
            yield
            cos, sin = _rope_expand(cos_ref[...], sin_ref[...])
            rope = lambda x: x * cos + pltpu.roll(x, ROPE_SWAP_ROLL, axis=1) * sin
            k_rope = rope(read_ref[:, EV_KR:EV_KR + LANES])
            for head in range(MLA_HEADS):
                sl = slice(head * HEAD_PAD, (head + 1) * HEAD_PAD)
                q_ref[:, sl] = (rope(q0[:, sl]) * QK_SCALE_LOG2E).astype(BF16)
                k_ref[:, sl] = (kn[:, sl] + k_rope).astype(BF16)
                yield

        _interleave(project(), finish())

    @pl.when(j % 2 == 0)
    def _():
        step(proj_even_ref, proj_odd_ref)

    @pl.when(j % 2 == 1)
    def _():
        step(proj_odd_ref, proj_even_ref)


def _even_pre(h, norm_g, w, cos, sin, batch, seq):
    t = batch * seq
    tm = ROW_BLOCK
    nblk = seq // tm
    n_steps = t // tm + 1
    full = lambda arr: pl.BlockSpec(arr.shape, lambda j: (0,) * arr.ndim)
    consts = [norm_g, w["w_in"], w["conv_w"], w["q_norm_g"], w["kv_norm_g"], w["w_q"], w["w_k"], w["w_v"]]
    prev = lambda j: jnp.maximum(j - 1, 0)
    block_in = pl.BlockSpec((tm, D_MODEL), lambda j: (jnp.minimum(j, n_steps - 2), 0))
    rows_prev = lambda width: pl.BlockSpec((tm, width), lambda j: (prev(j), 0))
    rope_prev = pl.BlockSpec((tm // ROPE_GROUPS, LANES), lambda j: (prev(j), 0))
    return pl.pallas_call(
        functools.partial(_even_pre_kernel, blocks_per_seq=nblk),
        grid=(n_steps,),
        in_specs=[block_in] + [full(a) for a in consts] + [rope_prev, rope_prev],
        out_specs=[rows_prev(SC_WIDTH), rows_prev(MLA_PAD), rows_prev(MLA_PAD),
                   pl.BlockSpec((None, None, MLA_PAD, tm), lambda j: (prev(j) // nblk, prev(j) % nblk, 0, 0))],
        out_shape=[jax.ShapeDtypeStruct((t, SC_WIDTH), BF16)] + [jax.ShapeDtypeStruct((t, MLA_PAD), BF16)] * 2
        + [jax.ShapeDtypeStruct((batch, nblk, MLA_PAD, tm), BF16)],
        scratch_shapes=[pltpu.VMEM((SUBLANES, SC_WIDTH), F32),
                        pltpu.VMEM((tm, EV_IN_PAD), F32),
                        pltpu.VMEM((tm, EV_IN_PAD), F32)],
        compiler_params=_params("arbitrary"),
        name="even_pre",
    )(h, *consts, cos, sin)


def _attn_kernel(q_ref, k_ref, v_ref, o_ref, acc_ref):
    tq = q_ref.shape[0]
    qi = pl.program_id(1)
    key = lax.broadcasted_iota(jnp.int32, (tq, tq), 0)
    query = lax.broadcasted_iota(jnp.int32, (tq, tq), 1)
    causal = key <= query
    feat = lax.broadcasted_iota(jnp.int32, (HEAD_PAD, tq), 0)

    def scores(head, j, masked):
        sl = slice(head * HEAD_PAD, (head + 1) * HEAD_PAD)
        start = pl.multiple_of(j * tq, tq)
        s = _dot_nt(k_ref[pl.ds(start, tq), sl], q_ref[:, sl])
        return jnp.where(causal, s, -jnp.inf) if masked else s

    def accumulate(head, j, m, s):
        sl = slice(head * HEAD_PAD, (head + 1) * HEAD_PAD)
        m_new = jnp.maximum(m, jnp.max(s, axis=0, keepdims=True))
        p = jnp.exp2(s - m_new)
        alpha = jnp.exp2(m - m_new)
        acc_ref[head] = alpha * acc_ref[head] + _dot(v_ref[j, sl, :], p.astype(BF16))
        return m_new

    def run(heads, tiles, ms):
        ms = list(ms)
        work = [(j, masked, idx) for j, masked in tiles for idx in range(len(heads))]
        pending = [scores(heads[idx], j, masked) for j, masked, idx in work[:ATTN_LOOKAHEAD]]
        for n, (j, masked, idx) in enumerate(work):
            if n + ATTN_LOOKAHEAD < len(work):
                j2, masked2, idx2 = work[n + ATTN_LOOKAHEAD]
                pending.append(scores(heads[idx2], j2, masked2))
            ms[idx] = accumulate(heads[idx], j, ms[idx], pending.pop(0))
        return tuple(ms)

    for group in range(MLA_HEADS // ATTN_HEAD_GROUP):
        heads = tuple(range(group * ATTN_HEAD_GROUP, (group + 1) * ATTN_HEAD_GROUP))
        for h in heads:
            acc_ref[h] = jnp.zeros((HEAD_PAD, tq), F32)
        m0 = jnp.full((1, tq), -jnp.inf, F32)
        ms = lax.fori_loop(0, qi // 2, lambda jj, ms: run(heads, [(2 * jj, False), (2 * jj + 1, False)], ms),
                           (m0,) * ATTN_HEAD_GROUP)

        @pl.when(qi % 2 == 1)
        def _():
            run(heads, [(qi - 1, False), (qi, True)], ms)

        @pl.when(qi % 2 == 0)
        def _():
            run(heads, [(qi, True)], ms)

        for even in heads[::2]:
            acc_even = acc_ref[even]
            acc_odd = acc_ref[even + 1]
            out = jnp.where(feat < MLA_V, acc_even / acc_even[MLA_V:MLA_V + 1, :], acc_odd / acc_odd[0:1, :])
            o_ref[:, (even // 2) * LANES:(even // 2 + 1) * LANES] = out.T.astype(BF16)


def _attention(q, k, vt, batch, seq):
    tq = ATTN_BLOCK
    q3, k3 = (a.reshape(batch, seq, MLA_PAD) for a in (q, k))
    out = pl.pallas_call(
        _attn_kernel,
        grid=(batch, seq // tq),
        in_specs=[pl.BlockSpec((None, tq, MLA_PAD), lambda b, i: (b, i, 0)),
                  pl.BlockSpec((None, seq, MLA_PAD), lambda b, i: (b, 0, 0)),
                  pl.BlockSpec((None,) + vt.shape[1:], lambda b, i: (b, 0, 0, 0))],
        out_specs=pl.BlockSpec((None, tq, MLA_HEADS * MLA_V), lambda b, i: (b, i, 0)),
        out_shape=jax.ShapeDtypeStruct((batch, seq, MLA_HEADS * MLA_V), BF16),
        scratch_shapes=[pltpu.VMEM((MLA_HEADS, HEAD_PAD, tq), F32)],
        compiler_params=_params("parallel", "arbitrary"),
        name="mla_attention",
    )(q3, k3, vt)
    return out.reshape(batch * seq, MLA_HEADS * MLA_V)


def _interleave(*stage_generators):
    live = list(stage_generators)
    while live:
        for gen in list(live):
            try:
                next(gen)
            except StopIteration:
                live.remove(gen)


def _gla_stages(q, k, v, g, og, gup_ref, gbias_ref, ng_ref, states, o_ref, rows):
    tb = q.shape[0]
    n_pairs = GLA_HEADS // 2
    pair_dk = 2 * GLA_DK
    pair_dv = 2 * GLA_DV
    n_chunks = tb // GLA_CHUNK

    gate = _dot(g.astype(BF16), gup_ref[...]) + gbias_ref[...]
    log_a = (jnp.minimum(gate, 0.0) - jnp.log1p(jnp.exp(-jnp.abs(gate)))) / GLA_TAU
    yield

    row = lax.broadcasted_iota(jnp.int32, (tb, tb), 0)
    col = lax.broadcasted_iota(jnp.int32, (tb, tb), 1)
    tril = ((row // GLA_CHUNK) == (col // GLA_CHUNK)) & (col <= row)
    tril_bf = tril.astype(BF16)
    cum = None
    rest = log_a
    for _ in range(3):
        piece = rest.astype(BF16)
        rest = rest - piece.astype(F32)
        part = _dot(tril_bf, piece)
        cum = part if cum is None else cum + part
    chunk_last = [cum[(c + 1) * GLA_CHUNK - 1:(c + 1) * GLA_CHUNK, :] for c in range(n_chunks)]
    last = jnp.concatenate([jnp.broadcast_to(t, (GLA_CHUNK, t.shape[1])) for t in chunk_last], axis=0)
    yield

    q_in = (q * (GLA_DK ** -0.5)) * jnp.exp(cum)
    k_in = k * jnp.exp(-cum)
    k_out = k * jnp.exp(last - cum)
    yield

    lane = lax.broadcasted_iota(jnp.int32, (tb, pair_dk), 1)
    state_rows = lax.broadcasted_iota(jnp.int32, (pair_dv, pair_dk), 0)
    state_lanes = lax.broadcasted_iota(jnp.int32, (pair_dv, pair_dk), 1)
    own_block = (state_rows < GLA_DV) == (state_lanes < GLA_DK)

    ksl = [slice(p * pair_dk, (p + 1) * pair_dk) for p in range(n_pairs)]
    vsl = [slice(p * pair_dv, (p + 1) * pair_dv) for p in range(n_pairs)]

    kvs = [[None] * n_chunks for _ in range(n_pairs)]
    for p in range(n_pairs):
        for c in range(n_chunks):
            crow = slice(c * GLA_CHUNK, (c + 1) * GLA_CHUNK)
            kvs[p][c] = jnp.where(own_block, _dot_tn(v[crow, vsl[p]], k_out[crow, ksl[p]].astype(BF16)), 0.0)
        yield
    entering = [[] for _ in range(n_pairs)]
    for c in range(n_chunks):
        for p in range(n_pairs):
            entering[p].append(states[p])
            states[p] = states[p] * jnp.exp(chunk_last[c][:, ksl[p]]) + kvs[p][c]
    yield

    for p in range(n_pairs):
        qp, kp = q_in[:, ksl[p]], k_in[:, ksl[p]]
        o_inter = jnp.concatenate(
            [_dot_nt(qp[c * GLA_CHUNK:(c + 1) * GLA_CHUNK].astype(BF16), entering[p][c].astype(BF16))
             for c in range(n_chunks)], axis=0)
        yield
        for sub in range(2):
            mine = (lane < GLA_DK) if sub == 0 else (lane >= GLA_DK)
            qh = jnp.where(mine, qp, 0.0).astype(BF16)
            kh = jnp.where(mine, kp, 0.0).astype(BF16)
            att = jnp.where(tril, _dot_nt(qh, kh), 0.0)
            head = 2 * p + sub
            hsl = slice(head * GLA_DV, (head + 1) * GLA_DV)
            o = _dot(att.astype(BF16), v[:, hsl]) + o_inter[:, sub * GLA_DV:(sub + 1) * GLA_DV]
            o = _rms(o) * ng_ref[...]
            gate_o = og[:, hsl]
            o_ref[rows, hsl] = (o * (gate_o * _sigmoid(gate_o))).astype(BF16)
            yield


def _lru_stages(x, y, cw_ref, cb_ref, wa_ref, ba_ref, wi_ref, bi_ref, lam_ref, tail_ref, carry_ref, o_ref,
                first_block):
    tb = x.shape[0]
    xc = _causal_conv(tail_ref, x, cw_ref, first_block) + cb_ref[...]
    xcb = xc.astype(BF16)
    yield

    groups = tb // SUBLANES
    sub = lax.broadcasted_iota(jnp.int32, (groups, SUBLANES, LRU_COL_SLAB), 1)
    for start in range(0, LRU_WIDTH, LRU_COL_SLAB):
        cols = slice(start, start + LRU_COL_SLAB)
        r = _sigmoid(_dot(xcb, wa_ref[:, cols]) + ba_ref[:, cols])
        yield
        gate_i = _sigmoid(_dot(xcb, wi_ref[:, cols]) + bi_ref[:, cols])
        yield
        log_a = (-LRU_C * r) * _softplus(-lam_ref[:, cols])
        a = jnp.exp(log_a)
        b = jnp.sqrt(-jnp.tanh(log_a) * (a * a + 1.0)) * (gate_i * xc[:, cols])
        yield

        a = a.reshape(groups, SUBLANES, LRU_COL_SLAB)
        b = b.reshape(groups, SUBLANES, LRU_COL_SLAB)
        shift = 1
        while shift < SUBLANES:
            a_prev = pltpu.roll(a, shift, axis=1)
            b_prev = pltpu.roll(b, shift, axis=1)
            take = sub >= shift
            b = jnp.where(take, a * b_prev + b, b)
            a = jnp.where(take, a * a_prev, a)
            shift *= 2
            yield

        h_prev = carry_ref[:, cols]
        hs = []
        for grp in range(groups):
            h = a[grp] * h_prev + b[grp]
            hs.append(h)
            h_prev = jnp.broadcast_to(h[SUBLANES - 1:SUBLANES, :], (SUBLANES, LRU_COL_SLAB))
            if grp % LRU_CHAIN_STAGE == LRU_CHAIN_STAGE - 1:
                yield
        carry_ref[:, cols] = h_prev
        hseq = jnp.stack(hs, axis=0).reshape(tb, LRU_COL_SLAB)
        ys = y[:, cols]
        gelu = ys * (0.5 * (1.0 + jnp.tanh(0.7978845608028654 * (ys + 0.044715 * (ys * ys * ys)))))
        o_ref[:, cols] = (gelu * hseq).astype(BF16)
        yield


def _odd_pre_kernel(h_ref, g_ref, win_ref, gup_ref, gbias_ref, ng_ref, cw_ref, cb_ref, wa_ref, ba_ref, wi_ref,
                    bi_ref, lam_ref, gla_ref, lru_ref, state_ref, tail_ref, carry_ref, proj_even_ref, proj_odd_ref,
                    *, blocks_per_seq):
    j = pl.program_id(0)
    width_k = GLA_HEADS * GLA_DK
    width_v = GLA_HEADS * GLA_DV

    @pl.when(j == 0)
    def _():
        proj_odd_ref[...] = jnp.zeros(proj_odd_ref.shape, F32)

    @pl.when((j == 0) | ((j - 1) % blocks_per_seq == 0))
    def _():
        state_ref[...] = jnp.zeros(state_ref.shape, F32)
        carry_ref[...] = jnp.zeros(carry_ref.shape, F32)
        tail_ref[...] = jnp.zeros(tail_ref.shape, F32)

    def step(write_ref, read_ref):
        hn = (_rms(h_ref[...]) * g_ref[...]).astype(BF16)

        def project():
            for start in range(0, OD_IN_PAD, ODD_PROJ_CHUNK):
                cols = slice(start, min(start + ODD_PROJ_CHUNK, OD_IN_PAD))
                write_ref[:, cols] = _dot(hn, win_ref[:, cols])
                yield

        states = [state_ref[p] for p in range(GLA_HEADS // 2)]

        gla_blocks = []
        for blk in range(h_ref.shape[0] // GLA_BLOCK):
            rows = slice(blk * GLA_BLOCK, (blk + 1) * GLA_BLOCK)
            gla_blocks.append(_gla_stages(
                read_ref[rows, OD_Q:OD_Q + width_k], read_ref[rows, OD_K:OD_K + width_k],
                read_ref[rows, OD_V:OD_V + width_v].astype(BF16), read_ref[rows, OD_G:OD_G + LANES],
                read_ref[rows, OD_OG:OD_OG + width_v], gup_ref, gbias_ref, ng_ref, states, gla_ref, rows))
        lru = _lru_stages(read_ref[:, OD_X:OD_X + LRU_WIDTH], read_ref[:, OD_Y:OD_Y + LRU_WIDTH], cw_ref, cb_ref,
                          wa_ref, ba_ref, wi_ref, bi_ref, lam_ref, tail_ref, carry_ref, lru_ref, None)
        _interleave(project(), lru, *gla_blocks)
        for p, s in enumerate(states):
            state_ref[p] = s

    @pl.when(j % 2 == 0)
    def _():
        step(proj_even_ref, proj_odd_ref)

    @pl.when(j % 2 == 1)
    def _():
        step(proj_odd_ref, proj_even_ref)


def _odd_pre(h, norm_g, w, batch, seq):
    t = batch * seq
    tm = ROW_BLOCK
    nblk = seq // tm
    n_steps = t // tm + 1
    full = lambda arr: pl.BlockSpec(arr.shape, lambda j: (0,) * arr.ndim)
    consts = [norm_g, w["w_in"], w["gate_up"], w["gate_bias"], w["gla_norm_g"], w["conv_w"], w["conv_b"], w["w_a"],
              w["b_a"], w["w_i"], w["b_i"], w["lam"]]
    width_v = GLA_HEADS * GLA_DV
    block_in = pl.BlockSpec((tm, D_MODEL), lambda j: (jnp.minimum(j, n_steps - 2), 0))
    block_out = lambda width: pl.BlockSpec((tm, width), lambda j: (jnp.maximum(j - 1, 0), 0))
    return pl.pallas_call(
        functools.partial(_odd_pre_kernel, blocks_per_seq=nblk),
        grid=(n_steps,),
        in_specs=[block_in] + [full(a) for a in consts],
        out_specs=[block_out(width_v), block_out(LRU_WIDTH)],
        out_shape=[jax.ShapeDtypeStruct((t, width_v), BF16), jax.ShapeDtypeStruct((t, LRU_WIDTH), BF16)],
        scratch_shapes=[pltpu.VMEM((GLA_HEADS // 2, 2 * GLA_DV, 2 * GLA_DK), F32),
                        pltpu.VMEM((SUBLANES, LRU_WIDTH), F32),
                        pltpu.VMEM((SUBLANES, LRU_WIDTH), F32),
                        pltpu.VMEM((tm, OD_IN_PAD), F32),
                        pltpu.VMEM((tm, OD_IN_PAD), F32)],
        compiler_params=_params("arbitrary"),
        name="odd_pre",
    )(h, *consts)


def _post_kernel(h_ref, a_ref, b_ref, wo_ref, g_ref, w1_ref, w2_ref, gf_ref, o_ref, *, final_norm):
    half = a_ref.shape[1]
    sub = h_ref.shape[0] // POST_ROW_SPLIT
    for r in range(POST_ROW_SPLIT):
        rows = slice(r * sub, (r + 1) * sub)
        mix = _dot(a_ref[rows, :], wo_ref[0:half, :]) + _dot(b_ref[rows, :], wo_ref[half:2 * half, :])
        h1 = h_ref[rows, :] + mix
        hn = (_rms(h1) * g_ref[...]).astype(BF16)
        out = h1
        for c in range(D_FF // FF_BLOCK):
            cols = slice(c * FF_BLOCK, (c + 1) * FF_BLOCK)
            hidden = _dot(hn, w1_ref[:, cols])
            act = jnp.square(jnp.maximum(hidden, 0.0)).astype(BF16)
            out = out + _dot(act, w2_ref[cols, :])
        if final_norm:
            out = _rms(out) * gf_ref[...]
        o_ref[rows, :] = out


def _post(h, a, b, w_out, out_layer, norm_g, w1, w2, mlp_layer, final_g, final_norm):
    t = h.shape[0]
    tm = POST_ROW_BLOCK
    rows = lambda width: pl.BlockSpec((tm, width), lambda i: (i, 0))
    vec = pl.BlockSpec((1, D_MODEL), lambda i: (0, 0))
    resident = lambda arr, layer: pl.BlockSpec((None,) + arr.shape[1:], lambda i: (layer, 0, 0),
                                               pipeline_mode=pl.Buffered(1))
    return pl.pallas_call(
        functools.partial(_post_kernel, final_norm=final_norm),
        grid=(t // tm,),
        in_specs=[rows(D_MODEL), rows(a.shape[1]), rows(b.shape[1]), resident(w_out, out_layer), vec,
                  resident(w1, mlp_layer), resident(w2, mlp_layer), vec],
        out_specs=rows(D_MODEL),
        out_shape=jax.ShapeDtypeStruct((t, D_MODEL), F32),
        compiler_params=_params("parallel"),
        name="post_mlp",
    )(h, a, b, w_out, norm_g, w1, w2, final_g)


def _even_weights(w_in, conv_w, q_norm_g, w_uq, kv_norm_g, w_ukv):
    half = MLA_ROPE // 2
    z = lambda *shape: jnp.zeros(shape, F32)
    kr = w_in[:, EV_KR:EV_KR + MLA_ROPE]
    kr1, kr2 = kr[:, :half], kr[:, half:]
    tail = HEAD_PAD - MLA_NOPE - MLA_ROPE - half
    kr_blk = jnp.concatenate([z(D_MODEL, MLA_NOPE), kr1, kr2, kr1, z(D_MODEL, tail)], axis=1)
    w_in_pad = jnp.concatenate([w_in[:, :EV_KR], kr_blk], axis=1)

    wq = w_uq.reshape(MLA_Q_RANK, MLA_HEADS, MLA_NOPE + MLA_ROPE)
    nope, r1, r2 = wq[..., :MLA_NOPE], wq[..., MLA_NOPE:MLA_NOPE + half], wq[..., MLA_NOPE + half:]
    zq = z(MLA_Q_RANK, MLA_HEADS, tail)
    w_q = jnp.concatenate([nope, r1, r2, r1, zq], axis=-1).reshape(MLA_Q_RANK, MLA_PAD)

    wkv = w_ukv.reshape(MLA_KV_RANK, MLA_HEADS, MLA_NOPE + MLA_V)
    k_nope, v = wkv[..., :MLA_NOPE], wkv[..., MLA_NOPE:]
    w_k = jnp.concatenate([k_nope, z(MLA_KV_RANK, MLA_HEADS, HEAD_PAD - MLA_NOPE)], axis=-1)
    zv = jnp.zeros_like(v)
    odd_head = (jnp.arange(MLA_HEADS) % 2 == 1)[None, :, None]
    w_v = jnp.concatenate([jnp.where(odd_head, zv, v), jnp.where(odd_head, v, zv)], axis=-1)
    return {
        "w_in": w_in_pad.astype(BF16), "conv_w": conv_w, "q_norm_g": q_norm_g[None, :],
        "kv_norm_g": kv_norm_g[None, :], "w_q": w_q.astype(BF16),
        "w_k": w_k.reshape(MLA_KV_RANK, MLA_PAD).astype(BF16),
        "w_v": w_v.reshape(MLA_KV_RANK, MLA_PAD).T.astype(BF16),
    }


def _block_diag(w):
    n, r, c = w.shape
    eye = jnp.eye(n, dtype=w.dtype)
    return (eye[:, None, :, None] * w[:, :, None, :]).reshape(n * r, n * c)


def _odd_weights(w_in, gate_up, gate_bias, gla_norm_g, conv_w, conv_b, w_a, b_a, w_i, b_i, lam):
    dk, dv = GLA_HEADS * GLA_DK, GLA_HEADS * GLA_DV
    o = 0
    parts = {}
    for name, width in (("q", dk), ("k", dk), ("v", dv), ("g", GLA_GATE_RANK), ("og", dv), ("x", LRU_WIDTH),
                        ("y", LRU_WIDTH)):
        parts[name] = w_in[:, o:o + width]
        o += width
    g_pad = jnp.zeros((D_MODEL, LANES - GLA_GATE_RANK), F32)
    w_in_pad = jnp.concatenate([parts["x"], parts["y"], parts["v"], parts["og"], parts["q"], parts["k"], parts["g"],
                                g_pad], axis=1)
    gate_up_pad = jnp.concatenate([gate_up, jnp.zeros((LANES - GLA_GATE_RANK, dk), F32)], axis=0)
    return {
        "w_in": w_in_pad.astype(BF16), "gate_up": gate_up_pad.astype(BF16), "gate_bias": gate_bias[None, :],
        "gla_norm_g": gla_norm_g[None, :], "conv_w": conv_w, "conv_b": conv_b[None, :],
        "w_a": _block_diag(w_a).astype(BF16), "b_a": b_a[None, :], "w_i": _block_diag(w_i).astype(BF16),
        "b_i": b_i[None, :], "lam": lam[None, :],
    }


def kernel(x, positions, mixer_norm_g, mlp_norm_g, mlp_w1, mlp_w2, final_norm_g, ev_w_in, ev_conv_w, mla_q_norm_g, mla_w_uq, mla_kv_norm_g, mla_w_ukv, ev_w_out, od_w_in, gla_w_gate_up, gla_b_gate, gla_norm_g, lru_conv_w, lru_conv_b, lru_w_a, lru_b_a, lru_w_i, lru_b_i, lru_lambda, od_w_out):
    batch, seq, d_model = x.shape
    depth = mixer_norm_g.shape[0]
    assert d_model == D_MODEL and seq % ROW_BLOCK == 0 and ROW_BLOCK % GLA_BLOCK == 0
    assert ATTN_BLOCK == ROW_BLOCK
    h = x.reshape(batch * seq, D_MODEL)
    cos, sin = _rope_tables(positions)
    final_g = final_norm_g[None, :]
    w1_bf, w2_bf = mlp_w1.astype(BF16), mlp_w2.astype(BF16)
    ev_w_out_bf, od_w_out_bf = ev_w_out.astype(BF16), od_w_out.astype(BF16)
    for layer in range(depth):
        j = layer // 2
        norm_g = mixer_norm_g[layer][None, :]
        if layer % 2 == 0:
            w = _even_weights(ev_w_in[j], ev_conv_w[j], mla_q_norm_g[j], mla_w_uq[j], mla_kv_norm_g[j], mla_w_ukv[j])
            mix_a, q, k, v = _even_pre(h, norm_g, w, cos, sin, batch, seq)
            mix_b = _attention(q, k, v, batch, seq)
            w_out = ev_w_out_bf
        else:
            w = _odd_weights(od_w_in[j], gla_w_gate_up[j], gla_b_gate[j], gla_norm_g[j], lru_conv_w[j],
                             lru_conv_b[j], lru_w_a[j], lru_b_a[j], lru_w_i[j], lru_b_i[j], lru_lambda[j])
            mix_a, mix_b = _odd_pre(h, norm_g, w, batch, seq)
            w_out = od_w_out_bf
        h = _post(h, mix_a, mix_b, w_out, j, mlp_norm_g[layer][None, :], w1_bf, w2_bf, layer, final_g,
                  final_norm=(layer == depth - 1))
    return h.reshape(batch, seq, D_MODEL)
```

```python
import functools

import jax
import jax.numpy as jnp
from jax import lax
from jax.experimental import pallas as pl
from jax.experimental.pallas import tpu as pltpu

F32 = jnp.float32
BF16 = jnp.bfloat16

D_MODEL = 1024
D_FF = 4 * D_MODEL
NORM_EPS = 1e-6

SC_WIDTH = 512
SC_KERNEL = 3
MLA_HEADS = 8
MLA_NOPE = 64
MLA_ROPE = 32
MLA_V = 64
MLA_Q_RANK = 384
MLA_KV_RANK = 256
ROPE_THETA = 10000.0
QK_SCALE_LOG2E = (MLA_NOPE + MLA_ROPE) ** -0.5 * 1.4426950408889634

GLA_HEADS = 4
GLA_DK = 64
GLA_DV = 128
GLA_GATE_RANK = 16
GLA_TAU = 16.0
GLA_CHUNK = 64
LRU_WIDTH = 512
LRU_BLOCKS = 8
LRU_KERNEL = 4
LRU_C = 8.0

LANES = 128
SUBLANES = 8
VMEM_LIMIT_BYTES = 56 * 1024 * 1024

HEAD_PAD = LANES
ROPE_SWAP_ROLL = LANES - MLA_ROPE // 2
ROPE_GROUPS = LANES // MLA_ROPE
ROPE_TABLE_BLOCK = 1024
MLA_PAD = MLA_HEADS * HEAD_PAD

EV_GB, EV_GC, EV_U = 0, 512, 1024
EV_CQ = 1536
EV_CKV = EV_CQ + MLA_Q_RANK
EV_KR = EV_CKV + MLA_KV_RANK
EV_IN_PAD = EV_KR + LANES

OD_X, OD_Y, OD_V, OD_OG = 0, 512, 1024, 1536
OD_Q, OD_K, OD_G = 2048, 2304, 2560
OD_IN_PAD = OD_G + LANES

ROW_BLOCK = 512
FF_BLOCK = 1024
POST_ROW_BLOCK = 1024
POST_ROW_SPLIT = 2
ATTN_BLOCK = 512
ATTN_HEAD_GROUP = 8
ATTN_LOOKAHEAD = 2
GLA_BLOCK = 256
LRU_CHAIN_STAGE = 8
ODD_PROJ_CHUNK = 512
EVEN_PROJ_CHUNK = 512
LRU_COL_SLAB = 512


def _rms(x):
    return x * lax.rsqrt(jnp.mean(x * x, axis=-1, keepdims=True) + NORM_EPS)


def _dot(a, b):
    return jnp.dot(a, b, preferred_element_type=F32)


def _dot_nt(a, b):
    return lax.dot_general(a, b, (((1,), (1,)), ((), ())), preferred_element_type=F32)


def _dot_tn(a, b):
    return lax.dot_general(a, b, (((0,), (0,)), ((), ())), preferred_element_type=F32)


def _sigmoid(x):
    return 0.5 * jnp.tanh(0.5 * x) + 0.5


def _softplus(x):
    return jnp.maximum(x, 0.0) + jnp.log1p(jnp.exp(-jnp.abs(x)))


def _params(*semantics, flags=None):
    return pltpu.CompilerParams(dimension_semantics=semantics, vmem_limit_bytes=VMEM_LIMIT_BYTES, flags=flags)


def _causal_conv(tail_ref, x, w_ref, first_block):
    rows, width = x.shape
    taps = w_ref.shape[0]
    groups = rows // SUBLANES

    if first_block is not None:
        @pl.when(first_block)
        def _():
            tail_ref[...] = jnp.zeros((SUBLANES, width), F32)

    x3 = x.reshape(groups, SUBLANES, width)
    with_tail = jnp.concatenate([tail_ref[...][None], x3], axis=0)
    tail_ref[...] = x3[groups - 1]
    sub = lax.broadcasted_iota(jnp.int32, (groups, SUBLANES, width), 1)
    out = w_ref[taps - 1:taps, :] * x3
    for k in range(taps - 1):
        shift = taps - 1 - k
        rolled = pltpu.roll(with_tail, shift, axis=1)
        shifted = jnp.where(sub >= shift, rolled[1:], rolled[:-1])
        out = out + w_ref[k:k + 1, :] * shifted
    return out.reshape(rows, width)


def _rope_table_kernel(pos_ref, freq_ref, sign_ref, cos_ref, sin_ref):
    ang = pos_ref[...].astype(F32) * freq_ref[...]
    cos_ref[...] = jnp.cos(ang)
    sin_ref[...] = jnp.sin(ang) * sign_ref[...]


def _rope_tables(positions):
    b, s = positions.shape
    t = b * s
    quarter = ROW_BLOCK // ROPE_GROUPS
    inv_freq = ROPE_THETA ** (-jnp.arange(0, MLA_ROPE, 2, dtype=F32) / MLA_ROPE)
    half = MLA_ROPE // 2
    freq = jnp.tile(jnp.concatenate([inv_freq, inv_freq]), ROPE_GROUPS)[None, :]
    sign = jnp.tile(jnp.concatenate([-jnp.ones((half,), F32), jnp.ones((half,), F32)]), ROPE_GROUPS)[None, :]
    pos = positions.reshape(t // ROW_BLOCK, ROPE_GROUPS, quarter).transpose(0, 2, 1)
    pos = jnp.repeat(pos, MLA_ROPE, axis=-1).reshape(t // ROPE_GROUPS, LANES)
    ts = min(ROPE_TABLE_BLOCK, t // ROPE_GROUPS)
    rows = pl.BlockSpec((ts, LANES), lambda i: (i, 0))
    lane = pl.BlockSpec((1, LANES), lambda i: (0, 0))
    return pl.pallas_call(
        _rope_table_kernel,
        grid=(t // ROPE_GROUPS // ts,),
        in_specs=[rows, lane, lane],
        out_specs=[rows, rows],
        out_shape=[jax.ShapeDtypeStruct((t // ROPE_GROUPS, LANES), F32)] * 2,
        compiler_params=_params("parallel"),
        name="rope_tables",
    )(pos, freq, sign)


def _rope_expand(cos_dense, sin_dense):
    lane = lax.broadcasted_iota(jnp.int32, cos_dense.shape, 1)
    rope_lanes = (lane >= MLA_NOPE) & (lane < MLA_NOPE + MLA_ROPE)
    base = jnp.where(lane < MLA_NOPE, 1.0, 0.0)
    cos, sin = [], []
    for group in range(ROPE_GROUPS):
        shift = (MLA_NOPE - MLA_ROPE * group) % LANES
        cos.append(jnp.where(rope_lanes, pltpu.roll(cos_dense, shift, axis=1), base))
        sin.append(jnp.where(rope_lanes, pltpu.roll(sin_dense, shift, axis=1), 0.0))
    return jnp.concatenate(cos, axis=0), jnp.concatenate(sin, axis=0)


def _even_pre_kernel(h_ref, g_ref, win_ref, cw_ref, qg_ref, kvg_ref, wq_ref, wk_ref, wv_ref, cos_ref, sin_ref,
                     sc_ref, q_ref, k_ref, v_ref, tail_ref, proj_even_ref, proj_odd_ref, *, blocks_per_seq):
    j = pl.program_id(0)

    @pl.when(j == 0)
    def _():
        proj_odd_ref[...] = jnp.zeros(proj_odd_ref.shape, F32)

    @pl.when((j == 0) | ((j - 1) % blocks_per_seq == 0))
    def _():
        tail_ref[...] = jnp.zeros(tail_ref.shape, F32)

    def step(write_ref, read_ref):
        hn = (_rms(h_ref[...]) * g_ref[...]).astype(BF16)

        def project():
            for start in range(0, EV_IN_PAD, EVEN_PROJ_CHUNK):
                cols = slice(start, min(start + EVEN_PROJ_CHUNK, EV_IN_PAD))
                write_ref[:, cols] = _dot(hn, win_ref[:, cols])
                yield

        def finish():
            gate_b = read_ref[:, EV_GB:EV_GB + SC_WIDTH]
            cu = read_ref[:, EV_GC:EV_GC + SC_WIDTH] * read_ref[:, EV_U:EV_U + SC_WIDTH]
            sc_ref[...] = (gate_b * _causal_conv(tail_ref, cu, cw_ref, None)).astype(BF16)
            yield
            cqn = (_rms(read_ref[:, EV_CQ:EV_CQ + MLA_Q_RANK]) * qg_ref[...]).astype(BF16)
            q0 = _dot(cqn, wq_ref[...])
            yield
            ckvn = (_rms(read_ref[:, EV_CKV:EV_CKV + MLA_KV_RANK]) * kvg_ref[...]).astype(BF16)
            kn = _dot(ckvn, wk_ref[...])
            yield
            vt = _dot_nt(wv_ref[...], ckvn)
            feat = lax.broadcasted_iota(jnp.int32, vt.shape, 0)
            ones_feat = (1 - (feat // HEAD_PAD) % 2) * MLA_V
            v_ref[...] = jnp.where(feat % HEAD_PAD == ones_feat, 1.0, vt).astype(BF16)
            yield
            cos, sin = _rope_expand(cos_ref[...], sin_ref[...])
            rope = lambda x: x * cos + pltpu.roll(x, ROPE_SWAP_ROLL, axis=1) * sin
            k_rope = rope(read_ref[:, EV_KR:EV_KR + LANES])
            for head in range(MLA_HEADS):
                sl = slice(head * HEAD_PAD, (head + 1) * HEAD_PAD)
                q_ref[:, sl] = (rope(q0[:, sl]) * QK_SCALE_LOG2E).astype(BF16)
                k_ref[:, sl] = (kn[:, sl] + k_rope).astype(BF16)
                yield

        _interleave(project(), finish())

    @pl.when(j % 2 == 0)
    def _():
        step(proj_even_ref, proj_odd_ref)

    @pl.when(j % 2 == 1)
    def _():
        step(proj_odd_ref, proj_even_ref)


def _even_pre(h, norm_g, w, cos, sin, batch, seq):
    t = batch * seq
    tm = ROW_BLOCK
    nblk = seq // tm
    n_steps = t // tm + 1
    full = lambda arr: pl.BlockSpec(arr.shape, lambda j: (0,) * arr.ndim)
    consts = [norm_g, w["w_in"], w["conv_w"], w["q_norm_g"], w["kv_norm_g"], w["w_q"], w["w_k"], w["w_v"]]
    prev = lambda j: jnp.maximum(j - 1, 0)
    block_in = pl.BlockSpec((tm, D_MODEL), lambda j: (jnp.minimum(j, n_steps - 2), 0))
    rows_prev = lambda width: pl.BlockSpec((tm, width), lambda j: (prev(j), 0))
    rope_prev = pl.BlockSpec((tm // ROPE_GROUPS, LANES), lambda j: (prev(j), 0))
    return pl.pallas_call(
        functools.partial(_even_pre_kernel, blocks_per_seq=nblk),
        grid=(n_steps,),
        in_specs=[block_in] + [full(a) for a in consts] + [rope_prev, rope_prev],
        out_specs=[rows_prev(SC_WIDTH), rows_prev(MLA_PAD), rows_prev(MLA_PAD),
                   pl.BlockSpec((None, None, MLA_PAD, tm), lambda j: (prev(j) // nblk, prev(j) % nblk, 0, 0))],
        out_shape=[jax.ShapeDtypeStruct((t, SC_WIDTH), BF16)] + [jax.ShapeDtypeStruct((t, MLA_PAD), BF16)] * 2
        + [jax.ShapeDtypeStruct((batch, nblk, MLA_PAD, tm), BF16)],
        scratch_shapes=[pltpu.VMEM((SUBLANES, SC_WIDTH), F32),
                        pltpu.VMEM((tm, EV_IN_PAD), F32),
                        pltpu.VMEM((tm, EV_IN_PAD), F32)],
        compiler_params=_params("arbitrary"),
        name="even_pre",
    )(h, *consts, cos, sin)


def _attn_kernel(q_ref, k_ref, v_ref, o_ref, acc_ref):
    tq = q_ref.shape[0]
    qi = pl.program_id(1)
    key = lax.broadcasted_iota(jnp.int32, (tq, tq), 0)
    query = lax.broadcasted_iota(jnp.int32, (tq, tq), 1)
    causal = key <= query
    feat = lax.broadcasted_iota(jnp.int32, (HEAD_PAD, tq), 0)

    def scores(head, j, masked):
        sl = slice(head * HEAD_PAD, (head + 1) * HEAD_PAD)
        start = pl.multiple_of(j * tq, tq)
        s = _dot_nt(k_ref[pl.ds(start, tq), sl], q_ref[:, sl])
        return jnp.where(causal, s, -jnp.inf) if masked else s

    def accumulate(head, j, m, s):
        sl = slice(head * HEAD_PAD, (head + 1) * HEAD_PAD)
        m_new = jnp.maximum(m, jnp.max(s, axis=0, keepdims=True))
        p = jnp.exp2(s - m_new)
        alpha = jnp.exp2(m - m_new)
        acc_ref[head] = alpha * acc_ref[head] + _dot(v_ref[j, sl, :], p.astype(BF16))
        return m_new

    def run(heads, tiles, ms):
        ms = list(ms)
        work = [(j, masked, idx) for j, masked in tiles for idx in range(len(heads))]
        pending = [scores(heads[idx], j, masked) for j, masked, idx in work[:ATTN_LOOKAHEAD]]
        for n, (j, masked, idx) in enumerate(work):
            if n + ATTN_LOOKAHEAD < len(work):
                j2, masked2, idx2 = work[n + ATTN_LOOKAHEAD]
                pending.append(scores(heads[idx2], j2, masked2))
            ms[idx] = accumulate(heads[idx], j, ms[idx], pending.pop(0))
        return tuple(ms)

    for group in range(MLA_HEADS // ATTN_HEAD_GROUP):
        heads = tuple(range(group * ATTN_HEAD_GROUP, (group + 1) * ATTN_HEAD_GROUP))
        for h in heads:
            acc_ref[h] = jnp.zeros((HEAD_PAD, tq), F32)
        m0 = jnp.full((1, tq), -jnp.inf, F32)
        ms = lax.fori_loop(0, qi // 2, lambda jj, ms: run(heads, [(2 * jj, False), (2 * jj + 1, False)], ms),
                           (m0,) * ATTN_HEAD_GROUP)

        @pl.when(qi % 2 == 1)
        def _():
            run(heads, [(qi - 1, False), (qi, True)], ms)

        @pl.when(qi % 2 == 0)
        def _():
            run(heads, [(qi, True)], ms)

        for even in heads[::2]:
            acc_even = acc_ref[even]
            acc_odd = acc_ref[even + 1]
            out = jnp.where(feat < MLA_V, acc_even / acc_even[MLA_V:MLA_V + 1, :], acc_odd / acc_odd[0:1, :])
            o_ref[:, (even // 2) * LANES:(even // 2 + 1) * LANES] = out.T.astype(BF16)


def _attention(q, k, vt, batch, seq):
    tq = ATTN_BLOCK
    q3, k3 = (a.reshape(batch, seq, MLA_PAD) for a in (q, k))
    out = pl.pallas_call(
        _attn_kernel,
        grid=(batch, seq // tq),
        in_specs=[pl.BlockSpec((None, tq, MLA_PAD), lambda b, i: (b, i, 0)),
                  pl.BlockSpec((None, seq, MLA_PAD), lambda b, i: (b, 0, 0)),
                  pl.BlockSpec((None,) + vt.shape[1:], lambda b, i: (b, 0, 0, 0))],
        out_specs=pl.BlockSpec((None, tq, MLA_HEADS * MLA_V), lambda b, i: (b, i, 0)),
        out_shape=jax.ShapeDtypeStruct((batch, seq, MLA_HEADS * MLA_V), BF16),
        scratch_shapes=[pltpu.VMEM((MLA_HEADS, HEAD_PAD, tq), F32)],
        compiler_params=_params("parallel", "arbitrary"),
        name="mla_attention",
    )(q3, k3, vt)
    return out.reshape(batch * seq, MLA_HEADS * MLA_V)


def _interleave(*stage_generators):
    live = list(stage_generators)
    while live:
        for gen in list(live):
            try:
                next(gen)
            except StopIteration:
                live.remove(gen)


def _gla_stages(q, k, v, g, og, gup_ref, gbias_ref, ng_ref, states, o_ref, rows):
    tb = q.shape[0]
    n_pairs = GLA_HEADS // 2
    pair_dk = 2 * GLA_DK
    pair_dv = 2 * GLA_DV
    n_chunks = tb // GLA_CHUNK

    gate = _dot(g.astype(BF16), gup_ref[...]) + gbias_ref[...]
    log_a = (jnp.minimum(gate, 0.0) - jnp.log1p(jnp.exp(-jnp.abs(gate)))) / GLA_TAU
    yield

    row = lax.broadcasted_iota(jnp.int32, (tb, tb), 0)
    col = lax.broadcasted_iota(jnp.int32, (tb, tb), 1)
    tril = ((row // GLA_CHUNK) == (col // GLA_CHUNK)) & (col <= row)
    tril_bf = tril.astype(BF16)
    cum = None
    rest = log_a
    for _ in range(3):
        piece = rest.astype(BF16)
        rest = rest - piece.astype(F32)
        part = _dot(tril_bf, piece)
        cum = part if cum is None else cum + part
    chunk_last = [cum[(c + 1) * GLA_CHUNK - 1:(c + 1) * GLA_CHUNK, :] for c in range(n_chunks)]
    last = jnp.concatenate([jnp.broadcast_to(t, (GLA_CHUNK, t.shape[1])) for t in chunk_last], axis=0)
    yield

    q_in = (q * (GLA_DK ** -0.5)) * jnp.exp(cum)
    k_in = k * jnp.exp(-cum)
    k_out = k * jnp.exp(last - cum)
    yield

    lane = lax.broadcasted_iota(jnp.int32, (tb, pair_dk), 1)
    state_rows = lax.broadcasted_iota(jnp.int32, (pair_dv, pair_dk), 0)
    state_lanes = lax.broadcasted_iota(jnp.int32, (pair_dv, pair_dk), 1)
    own_block = (state_rows < GLA_DV) == (state_lanes < GLA_DK)

    ksl = [slice(p * pair_dk, (p + 1) * pair_dk) for p in range(n_pairs)]
    vsl = [slice(p * pair_dv, (p + 1) * pair_dv) for p in range(n_pairs)]

    kvs = [[None] * n_chunks for _ in range(n_pairs)]
    for p in range(n_pairs):
        for c in range(n_chunks):
            crow = slice(c * GLA_CHUNK, (c + 1) * GLA_CHUNK)
            kvs[p][c] = jnp.where(own_block, _dot_tn(v[crow, vsl[p]], k_out[crow, ksl[p]].astype(BF16)), 0.0)
        yield
    entering = [[] for _ in range(n_pairs)]
    for c in range(n_chunks):
        for p in range(n_pairs):
            entering[p].append(states[p])
            states[p] = states[p] * jnp.exp(chunk_last[c][:, ksl[p]]) + kvs[p][c]
    yield

    for p in range(n_pairs):
        qp, kp = q_in[:, ksl[p]], k_in[:, ksl[p]]
        o_inter = jnp.concatenate(
            [_dot_nt(qp[c * GLA_CHUNK:(c + 1) * GLA_CHUNK].astype(BF16), entering[p][c].astype(BF16))
             for c in range(n_chunks)], axis=0)
        yield
        for sub in range(2):
            mine = (lane < GLA_DK) if sub == 0 else (lane >= GLA_DK)
            qh = jnp.where(mine, qp, 0.0).astype(BF16)
            kh = jnp.where(mine, kp, 0.0).astype(BF16)
            att = jnp.where(tril, _dot_nt(qh, kh), 0.0)
            head = 2 * p + sub
            hsl = slice(head * GLA_DV, (head + 1) * GLA_DV)
            o = _dot(att.astype(BF16), v[:, hsl]) + o_inter[:, sub * GLA_DV:(sub + 1) * GLA_DV]
            o = _rms(o) * ng_ref[...]
            gate_o = og[:, hsl]
            o_ref[rows, hsl] = (o * (gate_o * _sigmoid(gate_o))).astype(BF16)
            yield


def _lru_stages(x, y, cw_ref, cb_ref, wa_ref, ba_ref, wi_ref, bi_ref, lam_ref, tail_ref, carry_ref, o_ref,
                first_block):
    tb = x.shape[0]
    xc = _causal_conv(tail_ref, x, cw_ref, first_block) + cb_ref[...]
    xcb = xc.astype(BF16)
    yield

    groups = tb // SUBLANES
    sub = lax.broadcasted_iota(jnp.int32, (groups, SUBLANES, LRU_COL_SLAB), 1)
    for start in range(0, LRU_WIDTH, LRU_COL_SLAB):
        cols = slice(start, start + LRU_COL_SLAB)
        r = _sigmoid(_dot(xcb, wa_ref[:, cols]) + ba_ref[:, cols])
        yield
        gate_i = _sigmoid(_dot(xcb, wi_ref[:, cols]) + bi_ref[:, cols])
        yield
        log_a = (-LRU_C * r) * _softplus(-lam_ref[:, cols])
        a = jnp.exp(log_a)
        b = jnp.sqrt(-jnp.tanh(log_a) * (a * a + 1.0)) * (gate_i * xc[:, cols])
        yield

        a = a.reshape(groups, SUBLANES, LRU_COL_SLAB)
        b = b.reshape(groups, SUBLANES, LRU_COL_SLAB)
        shift = 1
        while shift < SUBLANES:
            a_prev = pltpu.roll(a, shift, axis=1)
            b_prev = pltpu.roll(b, shift, axis=1)
            take = sub >= shift
            b = jnp.where(take, a * b_prev + b, b)
            a = jnp.where(take, a * a_prev, a)
            shift *= 2
            yield

        h_prev = carry_ref[:, cols]
        hs = []
        for grp in range(groups):
            h = a[grp] * h_prev + b[grp]
            hs.append(h)
            h_prev = jnp.broadcast_to(h[SUBLANES - 1:SUBLANES, :], (SUBLANES, LRU_COL_SLAB))
            if grp % LRU_CHAIN_STAGE == LRU_CHAIN_STAGE - 1:
                yield
        carry_ref[:, cols] = h_prev
        hseq = jnp.stack(hs, axis=0).reshape(tb, LRU_COL_SLAB)
        ys = y[:, cols]
        gelu = ys * (0.5 * (1.0 + jnp.tanh(0.7978845608028654 * (ys + 0.044715 * (ys * ys * ys)))))
        o_ref[:, cols] = (gelu * hseq).astype(BF16)
        yield


def _odd_pre_kernel(h_ref, g_ref, win_ref, gup_ref, gbias_ref, ng_ref, cw_ref, cb_ref, wa_ref, ba_ref, wi_ref,
                    bi_ref, lam_ref, gla_ref, lru_ref, state_ref, tail_ref, carry_ref, proj_even_ref, proj_odd_ref,
                    *, blocks_per_seq):
    j = pl.program_id(0)
    width_k = GLA_HEADS * GLA_DK
    width_v = GLA_HEADS * GLA_DV

    @pl.when(j == 0)
    def _():
        proj_odd_ref[...] = jnp.zeros(proj_odd_ref.shape, F32)

    @pl.when((j == 0) | ((j - 1) % blocks_per_seq == 0))
    def _():
        state_ref[...] = jnp.zeros(state_ref.shape, F32)
        carry_ref[...] = jnp.zeros(carry_ref.shape, F32)
        tail_ref[...] = jnp.zeros(tail_ref.shape, F32)

    def step(write_ref, read_ref):
        hn = (_rms(h_ref[...]) * g_ref[...]).astype(BF16)

        def project():
            for start in range(0, OD_IN_PAD, ODD_PROJ_CHUNK):
                cols = slice(start, min(start + ODD_PROJ_CHUNK, OD_IN_PAD))
                write_ref[:, cols] = _dot(hn, win_ref[:, cols])
                yield

        states = [state_ref[p] for p in range(GLA_HEADS // 2)]

        gla_blocks = []
        for blk in range(h_ref.shape[0] // GLA_BLOCK):
            rows = slice(blk * GLA_BLOCK, (blk + 1) * GLA_BLOCK)
            gla_blocks.append(_gla_stages(
                read_ref[rows, OD_Q:OD_Q + width_k], read_ref[rows, OD_K:OD_K + width_k],
                read_ref[rows, OD_V:OD_V + width_v].astype(BF16), read_ref[rows, OD_G:OD_G + LANES],
                read_ref[rows, OD_OG:OD_OG + width_v], gup_ref, gbias_ref, ng_ref, states, gla_ref, rows))
        lru = _lru_stages(read_ref[:, OD_X:OD_X + LRU_WIDTH], read_ref[:, OD_Y:OD_Y + LRU_WIDTH], cw_ref, cb_ref,
                          wa_ref, ba_ref, wi_ref, bi_ref, lam_ref, tail_ref, carry_ref, lru_ref, None)
        _interleave(project(), lru, *gla_blocks)
        for p, s in enumerate(states):
            state_ref[p] = s

    @pl.when(j % 2 == 0)
    def _():
        step(proj_even_ref, proj_odd_ref)

    @pl.when(j % 2 == 1)
    def _():
        step(proj_odd_ref, proj_even_ref)


def _odd_pre(h, norm_g, w, batch, seq):
    t = batch * seq
    tm = ROW_BLOCK
    nblk = seq // tm
    n_steps = t // tm + 1
    full = lambda arr: pl.BlockSpec(arr.shape, lambda j: (0,) * arr.ndim)
    consts = [norm_g, w["w_in"], w["gate_up"], w["gate_bias"], w["gla_norm_g"], w["conv_w"], w["conv_b"], w["w_a"],
              w["b_a"], w["w_i"], w["b_i"], w["lam"]]
    width_v = GLA_HEADS * GLA_DV
    block_in = pl.BlockSpec((tm, D_MODEL), lambda j: (jnp.minimum(j, n_steps - 2), 0))
    block_out = lambda width: pl.BlockSpec((tm, width), lambda j: (jnp.maximum(j - 1, 0), 0))
    return pl.pallas_call(
        functools.partial(_odd_pre_kernel, blocks_per_seq=nblk),
        grid=(n_steps,),
        in_specs=[block_in] + [full(a) for a in consts],
        out_specs=[block_out(width_v), block_out(LRU_WIDTH)],
        out_shape=[jax.ShapeDtypeStruct((t, width_v), BF16), jax.ShapeDtypeStruct((t, LRU_WIDTH), BF16)],
        scratch_shapes=[pltpu.VMEM((GLA_HEADS // 2, 2 * GLA_DV, 2 * GLA_DK), F32),
                        pltpu.VMEM((SUBLANES, LRU_WIDTH), F32),
                        pltpu.VMEM((SUBLANES, LRU_WIDTH), F32),
                        pltpu.VMEM((tm, OD_IN_PAD), F32),
                        pltpu.VMEM((tm, OD_IN_PAD), F32)],
        compiler_params=_params("arbitrary"),
        name="odd_pre",
    )(h, *consts)


def _post_kernel(h_ref, a_ref, b_ref, wo_ref, g_ref, w1_ref, w2_ref, gf_ref, o_ref, *, final_norm):
    half = a_ref.shape[1]
    sub = h_ref.shape[0] // POST_ROW_SPLIT
    for r in range(POST_ROW_SPLIT):
        rows = slice(r * sub, (r + 1) * sub)
        mix = _dot(a_ref[rows, :], wo_ref[0:half, :]) + _dot(b_ref[rows, :], wo_ref[half:2 * half, :])
        h1 = h_ref[rows, :] + mix
        hn = (_rms(h1) * g_ref[...]).astype(BF16)
        out = h1
        for c in range(D_FF // FF_BLOCK):
            cols = slice(c * FF_BLOCK, (c + 1) * FF_BLOCK)
            hidden = _dot(hn, w1_ref[:, cols])
            act = jnp.square(jnp.maximum(hidden, 0.0)).astype(BF16)
            out = out + _dot(act, w2_ref[cols, :])
        if final_norm:
            out = _rms(out) * gf_ref[...]
        o_ref[rows, :] = out


def _post(h, a, b, w_out, out_layer, norm_g, w1, w2, mlp_layer, final_g, final_norm):
    t = h.shape[0]
    tm = POST_ROW_BLOCK
    rows = lambda width: pl.BlockSpec((tm, width), lambda i: (i, 0))
    vec = pl.BlockSpec((1, D_MODEL), lambda i: (0, 0))
    resident = lambda arr, layer: pl.BlockSpec((None,) + arr.shape[1:], lambda i: (layer, 0, 0),
                                               pipeline_mode=pl.Buffered(1))
    return pl.pallas_call(
        functools.partial(_post_kernel, final_norm=final_norm),
        grid=(t // tm,),
        in_specs=[rows(D_MODEL), rows(a.shape[1]), rows(b.shape[1]), resident(w_out, out_layer), vec,
                  resident(w1, mlp_layer), resident(w2, mlp_layer), vec],
        out_specs=rows(D_MODEL),
        out_shape=jax.ShapeDtypeStruct((t, D_MODEL), F32),
        compiler_params=_params("parallel"),
        name="post_mlp",
    )(h, a, b, w_out, norm_g, w1, w2, final_g)


def _even_weights(w_in, conv_w, q_norm_g, w_uq, kv_norm_g, w_ukv):
    half = MLA_ROPE // 2
    z = lambda *shape: jnp.zeros(shape, F32)
    kr = w_in[:, EV_KR:EV_KR + MLA_ROPE]
    kr1, kr2 = kr[:, :half], kr[:, half:]
    tail = HEAD_PAD - MLA_NOPE - MLA_ROPE - half
    kr_blk = jnp.concatenate([z(D_MODEL, MLA_NOPE), kr1, kr2, kr1, z(D_MODEL, tail)], axis=1)
    w_in_pad = jnp.concatenate([w_in[:, :EV_KR], kr_blk], axis=1)

    wq = w_uq.reshape(MLA_Q_RANK, MLA_HEADS, MLA_NOPE + MLA_ROPE)
    nope, r1, r2 = wq[..., :MLA_NOPE], wq[..., MLA_NOPE:MLA_NOPE + half], wq[..., MLA_NOPE + half:]
    zq = z(MLA_Q_RANK, MLA_HEADS, tail)
    w_q = jnp.concatenate([nope, r1, r2, r1, zq], axis=-1).reshape(MLA_Q_RANK, MLA_PAD)

    wkv = w_ukv.reshape(MLA_KV_RANK, MLA_HEADS, MLA_NOPE + MLA_V)
    k_nope, v = wkv[..., :MLA_NOPE], wkv[..., MLA_NOPE:]
    w_k = jnp.concatenate([k_nope, z(MLA_KV_RANK, MLA_HEADS, HEAD_PAD - MLA_NOPE)], axis=-1)
    zv = jnp.zeros_like(v)
    odd_head = (jnp.arange(MLA_HEADS) % 2 == 1)[None, :, None]
    w_v = jnp.concatenate([jnp.where(odd_head, zv, v), jnp.where(odd_head, v, zv)], axis=-1)
    return {
        "w_in": w_in_pad.astype(BF16), "conv_w": conv_w, "q_norm_g": q_norm_g[None, :],
        "kv_norm_g": kv_norm_g[None, :], "w_q": w_q.astype(BF16),
        "w_k": w_k.reshape(MLA_KV_RANK, MLA_PAD).astype(BF16),
        "w_v": w_v.reshape(MLA_KV_RANK, MLA_PAD).T.astype(BF16),
    }


def _block_diag(w):
    n, r, c = w.shape
    eye = jnp.eye(n, dtype=w.dtype)
    return (eye[:, None, :, None] * w[:, :, None, :]).reshape(n * r, n * c)


def _odd_weights(w_in, gate_up, gate_bias, gla_norm_g, conv_w, conv_b, w_a, b_a, w_i, b_i, lam):
    dk, dv = GLA_HEADS * GLA_DK, GLA_HEADS * GLA_DV
    o = 0
    parts = {}
    for name, width in (("q", dk), ("k", dk), ("v", dv), ("g", GLA_GATE_RANK), ("og", dv), ("x", LRU_WIDTH),
                        ("y", LRU_WIDTH)):
        parts[name] = w_in[:, o:o + width]
        o += width
    g_pad = jnp.zeros((D_MODEL, LANES - GLA_GATE_RANK), F32)
    w_in_pad = jnp.concatenate([parts["x"], parts["y"], parts["v"], parts["og"], parts["q"], parts["k"], parts["g"],
                                g_pad], axis=1)
    gate_up_pad = jnp.concatenate([gate_up, jnp.zeros((LANES - GLA_GATE_RANK, dk), F32)], axis=0)
    return {
        "w_in": w_in_pad.astype(BF16), "gate_up": gate_up_pad.astype(BF16), "gate_bias": gate_bias[None, :],
        "gla_norm_g": gla_norm_g[None, :], "conv_w": conv_w, "conv_b": conv_b[None, :],
        "w_a": _block_diag(w_a).astype(BF16), "b_a": b_a[None, :], "w_i": _block_diag(w_i).astype(BF16),
        "b_i": b_i[None, :], "lam": lam[None, :],
    }


def kernel(x, positions, mixer_norm_g, mlp_norm_g, mlp_w1, mlp_w2, final_norm_g, ev_w_in, ev_conv_w, mla_q_norm_g, mla_w_uq, mla_kv_norm_g, mla_w_ukv, ev_w_out, od_w_in, gla_w_gate_up, gla_b_gate, gla_norm_g, lru_conv_w, lru_conv_b, lru_w_a, lru_b_a, lru_w_i, lru_b_i, lru_lambda, od_w_out):
    batch, seq, d_model = x.shape
    depth = mixer_norm_g.shape[0]
    assert d_model == D_MODEL and seq % ROW_BLOCK == 0 and ROW_BLOCK % GLA_BLOCK == 0
    assert ATTN_BLOCK == ROW_BLOCK
    h = x.reshape(batch * seq, D_MODEL)
    cos, sin = _rope_tables(positions)
    final_g = final_norm_g[None, :]
    w1_bf, w2_bf = mlp_w1.astype(BF16), mlp_w2.astype(BF16)
    ev_w_out_bf, od_w_out_bf = ev_w_out.astype(BF16), od_w_out.astype(BF16)
    for layer in range(depth):
        j = layer // 2
        norm_g = mixer_norm_g[layer][None, :]
        if layer % 2 == 0:
            w = _even_weights(ev_w_in[j], ev_conv_w[j], mla_q_norm_g[j], mla_w_uq[j], mla_kv_norm_g[j], mla_w_ukv[j])
            mix_a, q, k, v = _even_pre(h, norm_g, w, cos, sin, batch, seq)
            mix_b = _attention(q, k, v, batch, seq)
            w_out = ev_w_out_bf
        else:
            w = _odd_weights(od_w_in[j], gla_w_gate_up[j], gla_b_gate[j], gla_norm_g[j], lru_conv_w[j],
                             lru_conv_b[j], lru_w_a[j], lru_b_a[j], lru_w_i[j], lru_b_i[j], lru_lambda[j])
            mix_a, mix_b = _odd_pre(h, norm_g, w, batch, seq)
            w_out = od_w_out_bf
        h = _post(h, mix_a, mix_b, w_out, j, mlp_norm_g[layer][None, :], w1_bf, w2_bf, layer, final_g,
                  final_norm=(layer == depth - 1))
    return h.reshape(batch, seq, D_MODEL)
```

```python
import functools

import jax
import jax.numpy as jnp
from jax import lax
from jax.experimental import pallas as pl
from jax.experimental.pallas import tpu as pltpu

F32 = jnp.float32
BF16 = jnp.bfloat16

D_MODEL = 1024
D_FF = 4 * D_MODEL
NORM_EPS = 1e-6

SC_WIDTH = 512
SC_KERNEL = 3
MLA_HEADS = 8
MLA_NOPE = 64
MLA_ROPE = 32
MLA_V = 64
MLA_Q_RANK = 384
MLA_KV_RANK = 256
ROPE_THETA = 10000.0
QK_SCALE_LOG2E = (MLA_NOPE + MLA_ROPE) ** -0.5 * 1.4426950408889634

GLA_HEADS = 4
GLA_DK = 64
GLA_DV = 128
GLA_GATE_RANK = 16
GLA_TAU = 16.0
GLA_CHUNK = 64
LRU_WIDTH = 512
LRU_BLOCKS = 8
LRU_KERNEL = 4
LRU_C = 8.0

LANES = 128
SUBLANES = 8
VMEM_LIMIT_BYTES = 56 * 1024 * 1024

HEAD_PAD = LANES
ROPE_SWAP_ROLL = LANES - MLA_ROPE // 2
ROPE_GROUPS = LANES // MLA_ROPE
ROPE_TABLE_BLOCK = 1024
MLA_PAD = MLA_HEADS * HEAD_PAD

EV_GB, EV_GC, EV_U = 0, 512, 1024
EV_CQ = 1536
EV_CKV = EV_CQ + MLA_Q_RANK
EV_KR = EV_CKV + MLA_KV_RANK
EV_IN_PAD = EV_KR + LANES

OD_X, OD_Y, OD_V, OD_OG = 0, 512, 1024, 1536
OD_Q, OD_K, OD_G = 2048, 2304, 2560
OD_IN_PAD = OD_G + LANES

ROW_BLOCK = 512
FF_BLOCK = 1024
POST_ROW_BLOCK = 1024
POST_ROW_SPLIT = 2
ATTN_BLOCK = 512
ATTN_HEAD_GROUP = 8
ATTN_LOOKAHEAD = 2
GLA_BLOCK = 256
LRU_CHAIN_STAGE = 8
ODD_PROJ_CHUNK = 512
EVEN_PROJ_CHUNK = 512
LRU_COL_SLAB = 512


def _rms(x):
    return x * lax.rsqrt(jnp.mean(x * x, axis=-1, keepdims=True) + NORM_EPS)


def _dot(a, b):
    return jnp.dot(a, b, preferred_element_type=F32)


def _dot_nt(a, b):
    return lax.dot_general(a, b, (((1,), (1,)), ((), ())), preferred_element_type=F32)


def _dot_tn(a, b):
    return lax.dot_general(a, b, (((0,), (0,)), ((), ())), preferred_element_type=F32)


def _sigmoid(x):
    return 0.5 * jnp.tanh(0.5 * x) + 0.5


def _softplus(x):
    return jnp.maximum(x, 0.0) + jnp.log1p(jnp.exp(-jnp.abs(x)))


def _params(*semantics, flags=None):
    return pltpu.CompilerParams(dimension_semantics=semantics, vmem_limit_bytes=VMEM_LIMIT_BYTES, flags=flags)


def _causal_conv(tail_ref, x, w_ref, first_block):
    rows, width = x.shape
    taps = w_ref.shape[0]
    groups = rows // SUBLANES

    if first_block is not None:
        @pl.when(first_block)
        def _():
            tail_ref[...] = jnp.zeros((SUBLANES, width), F32)

    x3 = x.reshape(groups, SUBLANES, width)
    with_tail = jnp.concatenate([tail_ref[...][None], x3], axis=0)
    tail_ref[...] = x3[groups - 1]
    sub = lax.broadcasted_iota(jnp.int32, (groups, SUBLANES, width), 1)
    out = w_ref[taps - 1:taps, :] * x3
    for k in range(taps - 1):
        shift = taps - 1 - k
        rolled = pltpu.roll(with_tail, shift, axis=1)
        shifted = jnp.where(sub >= shift, rolled[1:], rolled[:-1])
        out = out + w_ref[k:k + 1, :] * shifted
    return out.reshape(rows, width)


def _rope_table_kernel(pos_ref, freq_ref, sign_ref, cos_ref, sin_ref):
    ang = pos_ref[...].astype(F32) * freq_ref[...]
    cos_ref[...] = jnp.cos(ang)
    sin_ref[...] = jnp.sin(ang) * sign_ref[...]


def _rope_tables(positions):
    b, s = positions.shape
    t = b * s
    quarter = ROW_BLOCK // ROPE_GROUPS
    inv_freq = ROPE_THETA ** (-jnp.arange(0, MLA_ROPE, 2, dtype=F32) / MLA_ROPE)
    half = MLA_ROPE // 2
    freq = jnp.tile(jnp.concatenate([inv_freq, inv_freq]), ROPE_GROUPS)[None, :]
    sign = jnp.tile(jnp.concatenate([-jnp.ones((half,), F32), jnp.ones((half,), F32)]), ROPE_GROUPS)[None, :]
    pos = positions.reshape(t // ROW_BLOCK, ROPE_GROUPS, quarter).transpose(0, 2, 1)
    pos = jnp.repeat(pos, MLA_ROPE, axis=-1).reshape(t // ROPE_GROUPS, LANES)
    ts = min(ROPE_TABLE_BLOCK, t // ROPE_GROUPS)
    rows = pl.BlockSpec((ts, LANES), lambda i: (i, 0))
    lane = pl.BlockSpec((1, LANES), lambda i: (0, 0))
    return pl.pallas_call(
        _rope_table_kernel,
        grid=(t // ROPE_GROUPS // ts,),
        in_specs=[rows, lane, lane],
        out_specs=[rows, rows],
        out_shape=[jax.ShapeDtypeStruct((t // ROPE_GROUPS, LANES), F32)] * 2,
        compiler_params=_params("parallel"),
        name="rope_tables",
    )(pos, freq, sign)


def _rope_expand(cos_dense, sin_dense):
    lane = lax.broadcasted_iota(jnp.int32, cos_dense.shape, 1)
    rope_lanes = (lane >= MLA_NOPE) & (lane < MLA_NOPE + MLA_ROPE)
    base = jnp.where(lane < MLA_NOPE, 1.0, 0.0)
    cos, sin = [], []
    for group in range(ROPE_GROUPS):
        shift = (MLA_NOPE - MLA_ROPE * group) % LANES
        cos.append(jnp.where(rope_lanes, pltpu.roll(cos_dense, shift, axis=1), base))
        sin.append(jnp.where(rope_lanes, pltpu.roll(sin_dense, shift, axis=1), 0.0))
    return jnp.concatenate(cos, axis=0), jnp.concatenate(sin, axis=0)


def _even_pre_kernel(h_ref, g_ref, win_ref, cw_ref, qg_ref, kvg_ref, wq_ref, wk_ref, wv_ref, cos_ref, sin_ref,
                     sc_ref, q_ref, k_ref, v_ref, tail_ref, proj_even_ref, proj_odd_ref, *, blocks_per_seq):
    j = pl.program_id(0)

    @pl.when(j == 0)
    def _():
        proj_odd_ref[...] = jnp.zeros(proj_odd_ref.shape, F32)

    @pl.when((j == 0) | ((j - 1) % blocks_per_seq == 0))
    def _():
        tail_ref[...] = jnp.zeros(tail_ref.shape, F32)

    def step(write_ref, read_ref):
        hn = (_rms(h_ref[...]) * g_ref[...]).astype(BF16)

        def project():
            for start in range(0, EV_IN_PAD, EVEN_PROJ_CHUNK):
                cols = slice(start, min(start + EVEN_PROJ_CHUNK, EV_IN_PAD))
                write_ref[:, cols] = _dot(hn, win_ref[:, cols])
                yield

        def finish():
            gate_b = read_ref[:, EV_GB:EV_GB + SC_WIDTH]
            cu = read_ref[:, EV_GC:EV_GC + SC_WIDTH] * read_ref[:, EV_U:EV_U + SC_WIDTH]
            sc_ref[...] = (gate_b * _causal_conv(tail_ref, cu, cw_ref, None)).astype(BF16)
            yield
            cqn = (_rms(read_ref[:, EV_CQ:EV_CQ + MLA_Q_RANK]) * qg_ref[...]).astype(BF16)
            q0 = _dot(cqn, wq_ref[...])
            yield
            ckvn = (_rms(read_ref[:, EV_CKV:EV_CKV + MLA_KV_RANK]) * kvg_ref[...]).astype(BF16)
            kn = _dot(ckvn, wk_ref[...])
            yield
            vt = _dot_nt(wv_ref[...], ckvn)
            feat = lax.broadcasted_iota(jnp.int32, vt.shape, 0)
            ones_feat = (1 - (feat // HEAD_PAD) % 2) * MLA_V
            v_ref[...] = jnp.where(feat % HEAD_PAD == ones_feat, 1.0, vt).astype(BF16)
            yield
            cos, sin = _rope_expand(cos_ref[...], sin_ref[...])
            rope = lambda x: x * cos + pltpu.roll(x, ROPE_SWAP_ROLL, axis=1) * sin
            k_rope = rope(read_ref[:, EV_KR:EV_KR + LANES])
            for head in range(MLA_HEADS):
                sl = slice(head * HEAD_PAD, (head + 1) * HEAD_PAD)
                q_ref[:, sl] = (rope(q0[:, sl]) * QK_SCALE_LOG2E).astype(BF16)
                k_ref[:, sl] = (kn[:, sl] + k_rope).astype(BF16)
                yield

        _interleave(project(), finish())

    @pl.when(j % 2 == 0)
    def _():
        step(proj_even_ref, proj_odd_ref)

    @pl.when(j % 2 == 1)
    def _():
        step(proj_odd_ref, proj_even_ref)


def _even_pre(h, norm_g, w, cos, sin, batch, seq):
    t = batch * seq
    tm = ROW_BLOCK
    nblk = seq // tm
    n_steps = t // tm + 1
    full = lambda arr: pl.BlockSpec(arr.shape, lambda j: (0,) * arr.ndim)
    consts = [norm_g, w["w_in"], w["conv_w"], w["q_norm_g"], w["kv_norm_g"], w["w_q"], w["w_k"], w["w_v"]]
    prev = lambda j: jnp.maximum(j - 1, 0)
    block_in = pl.BlockSpec((tm, D_MODEL), lambda j: (jnp.minimum(j, n_steps - 2), 0))
    rows_prev = lambda width: pl.BlockSpec((tm, width), lambda j: (prev(j), 0))
    rope_prev = pl.BlockSpec((tm // ROPE_GROUPS, LANES), lambda j: (prev(j), 0))
    return pl.pallas_call(
        functools.partial(_even_pre_kernel, blocks_per_seq=nblk),
        grid=(n_steps,),
        in_specs=[block_in] + [full(a) for a in consts] + [rope_prev, rope_prev],
        out_specs=[rows_prev(SC_WIDTH), rows_prev(MLA_PAD), rows_prev(MLA_PAD),
                   pl.BlockSpec((None, None, MLA_PAD, tm), lambda j: (prev(j) // nblk, prev(j) % nblk, 0, 0))],
        out_shape=[jax.ShapeDtypeStruct((t, SC_WIDTH), BF16)] + [jax.ShapeDtypeStruct((t, MLA_PAD), BF16)] * 2
        + [jax.ShapeDtypeStruct((batch, nblk, MLA_PAD, tm), BF16)],
        scratch_shapes=[pltpu.VMEM((SUBLANES, SC_WIDTH), F32),
                        pltpu.VMEM((tm, EV_IN_PAD), F32),
                        pltpu.VMEM((tm, EV_IN_PAD), F32)],
        compiler_params=_params("arbitrary"),
        name="even_pre",
    )(h, *consts, cos, sin)


def _attn_kernel(q_ref, k_ref, v_ref, o_ref, acc_ref):
    tq = q_ref.shape[0]
    qi = pl.program_id(1)
    key = lax.broadcasted_iota(jnp.int32, (tq, tq), 0)
    query = lax.broadcasted_iota(jnp.int32, (tq, tq), 1)
    causal = key <= query
    feat = lax.broadcasted_iota(jnp.int32, (HEAD_PAD, tq), 0)

    def scores(head, j, masked):
        sl = slice(head * HEAD_PAD, (head + 1) * HEAD_PAD)
        start = pl.multiple_of(j * tq, tq)
        s = _dot_nt(k_ref[pl.ds(start, tq), sl], q_ref[:, sl])
        return jnp.where(causal, s, -jnp.inf) if masked else s

    def accumulate(head, j, m, s):
        sl = slice(head * HEAD_PAD, (head + 1) * HEAD_PAD)
        m_new = jnp.maximum(m, jnp.max(s, axis=0, keepdims=True))
        p = jnp.exp2(s - m_new)
        alpha = jnp.exp2(m - m_new)
        acc_ref[head] = alpha * acc_ref[head] + _dot(v_ref[j, sl, :], p.astype(BF16))
        return m_new

    def run(heads, tiles, ms):
        ms = list(ms)
        work = [(j, masked, idx) for j, masked in tiles for idx in range(len(heads))]
        pending = [scores(heads[idx], j, masked) for j, masked, idx in work[:ATTN_LOOKAHEAD]]
        for n, (j, masked, idx) in enumerate(work):
            if n + ATTN_LOOKAHEAD < len(work):
                j2, masked2, idx2 = work[n + ATTN_LOOKAHEAD]
                pending.append(scores(heads[idx2], j2, masked2))
            ms[idx] = accumulate(heads[idx], j, ms[idx], pending.pop(0))
        return tuple(ms)

    for group in range(MLA_HEADS // ATTN_HEAD_GROUP):
        heads = tuple(range(group * ATTN_HEAD_GROUP, (group + 1) * ATTN_HEAD_GROUP))
        for h in heads:
            acc_ref[h] = jnp.zeros((HEAD_PAD, tq), F32)
        m0 = jnp.full((1, tq), -jnp.inf, F32)
        ms = lax.fori_loop(0, qi // 2, lambda jj, ms: run(heads, [(2 * jj, False), (2 * jj + 1, False)], ms),
                           (m0,) * ATTN_HEAD_GROUP)

        @pl.when(qi % 2 == 1)
        def _():
            run(heads, [(qi - 1, False), (qi, True)], ms)

        @pl.when(qi % 2 == 0)
        def _():
            run(heads, [(qi, True)], ms)

        for even in heads[::2]:
            acc_even = acc_ref[even]
            acc_odd = acc_ref[even + 1]
            out = jnp.where(feat < MLA_V, acc_even / acc_even[MLA_V:MLA_V + 1, :], acc_odd / acc_odd[0:1, :])
            o_ref[:, (even // 2) * LANES:(even // 2 + 1) * LANES] = out.T.astype(BF16)


def _attention(q, k, vt, batch, seq):
    tq = ATTN_BLOCK
    q3, k3 = (a.reshape(batch, seq, MLA_PAD) for a in (q, k))
    out = pl.pallas_call(
        _attn_kernel,
        grid=(batch, seq // tq),
        in_specs=[pl.BlockSpec((None, tq, MLA_PAD), lambda b, i: (b, i, 0)),
                  pl.BlockSpec((None, seq, MLA_PAD), lambda b, i: (b, 0, 0), pipeline_mode=pl.Buffered(1)),
                  pl.BlockSpec((None,) + vt.shape[1:], lambda b, i: (b, 0, 0, 0), pipeline_mode=pl.Buffered(1))],
        out_specs=pl.BlockSpec((None, tq, MLA_HEADS * MLA_V), lambda b, i: (b, i, 0)),
        out_shape=jax.ShapeDtypeStruct((batch, seq, MLA_HEADS * MLA_V), BF16),
        scratch_shapes=[pltpu.VMEM((MLA_HEADS, HEAD_PAD, tq), F32)],
        compiler_params=_params("parallel", "arbitrary"),
        name="mla_attention",
    )(q3, k3, vt)
    return out.reshape(batch * seq, MLA_HEADS * MLA_V)


def _interleave(*stage_generators):
    live = list(stage_generators)
    while live:
        for gen in list(live):
            try:
                next(gen)
            except StopIteration:
                live.remove(gen)


def _gla_stages(q, k, v, g, og, gup_ref, gbias_ref, ng_ref, states, o_ref, rows):
    tb = q.shape[0]
    n_pairs = GLA_HEADS // 2
    pair_dk = 2 * GLA_DK
    pair_dv = 2 * GLA_DV
    n_chunks = tb // GLA_CHUNK

    gate = _dot(g.astype(BF16), gup_ref[...]) + gbias_ref[...]
    log_a = (jnp.minimum(gate, 0.0) - jnp.log1p(jnp.exp(-jnp.abs(gate)))) / GLA_TAU
    yield

    row = lax.broadcasted_iota(jnp.int32, (tb, tb), 0)
    col = lax.broadcasted_iota(jnp.int32, (tb, tb), 1)
    tril = ((row // GLA_CHUNK) == (col // GLA_CHUNK)) & (col <= row)
    tril_bf = tril.astype(BF16)
    cum = None
    rest = log_a
    for _ in range(3):
        piece = rest.astype(BF16)
        rest = rest - piece.astype(F32)
        part = _dot(tril_bf, piece)
        cum = part if cum is None else cum + part
    chunk_last = [cum[(c + 1) * GLA_CHUNK - 1:(c + 1) * GLA_CHUNK, :] for c in range(n_chunks)]
    last = jnp.concatenate([jnp.broadcast_to(t, (GLA_CHUNK, t.shape[1])) for t in chunk_last], axis=0)
    yield

    q_in = (q * (GLA_DK ** -0.5)) * jnp.exp(cum)
    k_in = k * jnp.exp(-cum)
    k_out = k * jnp.exp(last - cum)
    yield

    lane = lax.broadcasted_iota(jnp.int32, (tb, pair_dk), 1)
    state_rows = lax.broadcasted_iota(jnp.int32, (pair_dv, pair_dk), 0)
    state_lanes = lax.broadcasted_iota(jnp.int32, (pair_dv, pair_dk), 1)
    own_block = (state_rows < GLA_DV) == (state_lanes < GLA_DK)

    ksl = [slice(p * pair_dk, (p + 1) * pair_dk) for p in range(n_pairs)]
    vsl = [slice(p * pair_dv, (p + 1) * pair_dv) for p in range(n_pairs)]

    kvs = [[None] * n_chunks for _ in range(n_pairs)]
    for p in range(n_pairs):
        for c in range(n_chunks):
            crow = slice(c * GLA_CHUNK, (c + 1) * GLA_CHUNK)
            kvs[p][c] = jnp.where(own_block, _dot_tn(v[crow, vsl[p]], k_out[crow, ksl[p]].astype(BF16)), 0.0)
        yield
    entering = [[] for _ in range(n_pairs)]
    for c in range(n_chunks):
        for p in range(n_pairs):
            entering[p].append(states[p])
            states[p] = states[p] * jnp.exp(chunk_last[c][:, ksl[p]]) + kvs[p][c]
    yield

    for p in range(n_pairs):
        qp, kp = q_in[:, ksl[p]], k_in[:, ksl[p]]
        o_inter = jnp.concatenate(
            [_dot_nt(qp[c * GLA_CHUNK:(c + 1) * GLA_CHUNK].astype(BF16), entering[p][c].astype(BF16))
             for c in range(n_chunks)], axis=0)
        yield
        for sub in range(2):
            mine = (lane < GLA_DK) if sub == 0 else (lane >= GLA_DK)
            qh = jnp.where(mine, qp, 0.0).astype(BF16)
            kh = jnp.where(mine, kp, 0.0).astype(BF16)
            att = jnp.where(tril, _dot_nt(qh, kh), 0.0)
            head = 2 * p + sub
            hsl = slice(head * GLA_DV, (head + 1) * GLA_DV)
            o = _dot(att.astype(BF16), v[:, hsl]) + o_inter[:, sub * GLA_DV:(sub + 1) * GLA_DV]
            o = _rms(o) * ng_ref[...]
            gate_o = og[:, hsl]
            o_ref[rows, hsl] = (o * (gate_o * _sigmoid(gate_o))).astype(BF16)
            yield


def _lru_stages(x, y, cw_ref, cb_ref, wa_ref, ba_ref, wi_ref, bi_ref, lam_ref, tail_ref, carry_ref, o_ref,
                first_block):
    tb = x.shape[0]
    xc = _causal_conv(tail_ref, x, cw_ref, first_block) + cb_ref[...]
    xcb = xc.astype(BF16)
    yield

    groups = tb // SUBLANES
    sub = lax.broadcasted_iota(jnp.int32, (groups, SUBLANES, LRU_COL_SLAB), 1)
    for start in range(0, LRU_WIDTH, LRU_COL_SLAB):
        cols = slice(start, start + LRU_COL_SLAB)
        r = _sigmoid(_dot(xcb, wa_ref[:, cols]) + ba_ref[:, cols])
        yield
        gate_i = _sigmoid(_dot(xcb, wi_ref[:, cols]) + bi_ref[:, cols])
        yield
        log_a = (-LRU_C * r) * _softplus(-lam_ref[:, cols])
        a = jnp.exp(log_a)
        b = jnp.sqrt(-jnp.tanh(log_a) * (a * a + 1.0)) * (gate_i * xc[:, cols])
        yield

        a = a.reshape(groups, SUBLANES, LRU_COL_SLAB)
        b = b.reshape(groups, SUBLANES, LRU_COL_SLAB)
        shift = 1
        while shift < SUBLANES:
            a_prev = pltpu.roll(a, shift, axis=1)
            b_prev = pltpu.roll(b, shift, axis=1)
            take = sub >= shift
            b = jnp.where(take, a * b_prev + b, b)
            a = jnp.where(take, a * a_prev, a)
            shift *= 2
            yield

        h_prev = carry_ref[:, cols]
        hs = []
        for grp in range(groups):
            h = a[grp] * h_prev + b[grp]
            hs.append(h)
            h_prev = jnp.broadcast_to(h[SUBLANES - 1:SUBLANES, :], (SUBLANES, LRU_COL_SLAB))
            if grp % LRU_CHAIN_STAGE == LRU_CHAIN_STAGE - 1:
                yield
        carry_ref[:, cols] = h_prev
        hseq = jnp.stack(hs, axis=0).reshape(tb, LRU_COL_SLAB)
        ys = y[:, cols]
        gelu = ys * (0.5 * (1.0 + jnp.tanh(0.7978845608028654 * (ys + 0.044715 * (ys * ys * ys)))))
        o_ref[:, cols] = (gelu * hseq).astype(BF16)
        yield


def _odd_pre_kernel(h_ref, g_ref, win_ref, gup_ref, gbias_ref, ng_ref, cw_ref, cb_ref, wa_ref, ba_ref, wi_ref,
                    bi_ref, lam_ref, gla_ref, lru_ref, state_ref, tail_ref, carry_ref, proj_even_ref, proj_odd_ref,
                    *, blocks_per_seq):
    j = pl.program_id(0)
    width_k = GLA_HEADS * GLA_DK
    width_v = GLA_HEADS * GLA_DV

    @pl.when(j == 0)
    def _():
        proj_odd_ref[...] = jnp.zeros(proj_odd_ref.shape, F32)

    @pl.when((j == 0) | ((j - 1) % blocks_per_seq == 0))
    def _():
        state_ref[...] = jnp.zeros(state_ref.shape, F32)
        carry_ref[...] = jnp.zeros(carry_ref.shape, F32)
        tail_ref[...] = jnp.zeros(tail_ref.shape, F32)

    def step(write_ref, read_ref):
        hn = (_rms(h_ref[...]) * g_ref[...]).astype(BF16)

        def project():
            for start in range(0, OD_IN_PAD, ODD_PROJ_CHUNK):
                cols = slice(start, min(start + ODD_PROJ_CHUNK, OD_IN_PAD))
                write_ref[:, cols] = _dot(hn, win_ref[:, cols])
                yield

        states = [state_ref[p] for p in range(GLA_HEADS // 2)]

        gla_blocks = []
        for blk in range(h_ref.shape[0] // GLA_BLOCK):
            rows = slice(blk * GLA_BLOCK, (blk + 1) * GLA_BLOCK)
            gla_blocks.append(_gla_stages(
                read_ref[rows, OD_Q:OD_Q + width_k], read_ref[rows, OD_K:OD_K + width_k],
                read_ref[rows, OD_V:OD_V + width_v].astype(BF16), read_ref[rows, OD_G:OD_G + LANES],
                read_ref[rows, OD_OG:OD_OG + width_v], gup_ref, gbias_ref, ng_ref, states, gla_ref, rows))
        lru = _lru_stages(read_ref[:, OD_X:OD_X + LRU_WIDTH], read_ref[:, OD_Y:OD_Y + LRU_WIDTH], cw_ref, cb_ref,
                          wa_ref, ba_ref, wi_ref, bi_ref, lam_ref, tail_ref, carry_ref, lru_ref, None)
        _interleave(project(), lru, *gla_blocks)
        for p, s in enumerate(states):
            state_ref[p] = s

    @pl.when(j % 2 == 0)
    def _():
        step(proj_even_ref, proj_odd_ref)

    @pl.when(j % 2 == 1)
    def _():
        step(proj_odd_ref, proj_even_ref)


def _odd_pre(h, norm_g, w, batch, seq):
    t = batch * seq
    tm = ROW_BLOCK
    nblk = seq // tm
    n_steps = t // tm + 1
    full = lambda arr: pl.BlockSpec(arr.shape, lambda j: (0,) * arr.ndim)
    consts = [norm_g, w["w_in"], w["gate_up"], w["gate_bias"], w["gla_norm_g"], w["conv_w"], w["conv_b"], w["w_a"],
              w["b_a"], w["w_i"], w["b_i"], w["lam"]]
    width_v = GLA_HEADS * GLA_DV
    block_in = pl.BlockSpec((tm, D_MODEL), lambda j: (jnp.minimum(j, n_steps - 2), 0))
    block_out = lambda width: pl.BlockSpec((tm, width), lambda j: (jnp.maximum(j - 1, 0), 0))
    return pl.pallas_call(
        functools.partial(_odd_pre_kernel, blocks_per_seq=nblk),
        grid=(n_steps,),
        in_specs=[block_in] + [full(a) for a in consts],
        out_specs=[block_out(width_v), block_out(LRU_WIDTH)],
        out_shape=[jax.ShapeDtypeStruct((t, width_v), BF16), jax.ShapeDtypeStruct((t, LRU_WIDTH), BF16)],
        scratch_shapes=[pltpu.VMEM((GLA_HEADS // 2, 2 * GLA_DV, 2 * GLA_DK), F32),
                        pltpu.VMEM((SUBLANES, LRU_WIDTH), F32),
                        pltpu.VMEM((SUBLANES, LRU_WIDTH), F32),
                        pltpu.VMEM((tm, OD_IN_PAD), F32),
                        pltpu.VMEM((tm, OD_IN_PAD), F32)],
        compiler_params=_params("arbitrary"),
        name="odd_pre",
    )(h, *consts)


def _post_kernel(h_ref, a_ref, b_ref, wo_ref, g_ref, w1_ref, w2_ref, gf_ref, o_ref, *, final_norm):
    half = a_ref.shape[1]
    sub = h_ref.shape[0] // POST_ROW_SPLIT
    for r in range(POST_ROW_SPLIT):
        rows = slice(r * sub, (r + 1) * sub)
        mix = _dot(a_ref[rows, :], wo_ref[0:half, :]) + _dot(b_ref[rows, :], wo_ref[half:2 * half, :])
        h1 = h_ref[rows, :] + mix
        hn = (_rms(h1) * g_ref[...]).astype(BF16)
        out = h1
        for c in range(D_FF // FF_BLOCK):
            cols = slice(c * FF_BLOCK, (c + 1) * FF_BLOCK)
            hidden = _dot(hn, w1_ref[:, cols])
            act = jnp.square(jnp.maximum(hidden, 0.0)).astype(BF16)
            out = out + _dot(act, w2_ref[cols, :])
        if final_norm:
            out = _rms(out) * gf_ref[...]
        o_ref[rows, :] = out


def _post(h, a, b, w_out, out_layer, norm_g, w1, w2, mlp_layer, final_g, final_norm):
    t = h.shape[0]
    tm = POST_ROW_BLOCK
    rows = lambda width: pl.BlockSpec((tm, width), lambda i: (i, 0))
    vec = pl.BlockSpec((1, D_MODEL), lambda i: (0, 0))
    resident = lambda arr, layer: pl.BlockSpec((None,) + arr.shape[1:], lambda i: (layer, 0, 0),
                                               pipeline_mode=pl.Buffered(1))
    return pl.pallas_call(
        functools.partial(_post_kernel, final_norm=final_norm),
        grid=(t // tm,),
        in_specs=[rows(D_MODEL), rows(a.shape[1]), rows(b.shape[1]), resident(w_out, out_layer), vec,
                  resident(w1, mlp_layer), resident(w2, mlp_layer), vec],
        out_specs=rows(D_MODEL),
        out_shape=jax.ShapeDtypeStruct((t, D_MODEL), F32),
        compiler_params=_params("parallel"),
        name="post_mlp",
    )(h, a, b, w_out, norm_g, w1, w2, final_g)


def _even_weights(w_in, conv_w, q_norm_g, w_uq, kv_norm_g, w_ukv):
    half = MLA_ROPE // 2
    z = lambda *shape: jnp.zeros(shape, F32)
    kr = w_in[:, EV_KR:EV_KR + MLA_ROPE]
    kr1, kr2 = kr[:, :half], kr[:, half:]
    tail = HEAD_PAD - MLA_NOPE - MLA_ROPE - half
    kr_blk = jnp.concatenate([z(D_MODEL, MLA_NOPE), kr1, kr2, kr1, z(D_MODEL, tail)], axis=1)
    w_in_pad = jnp.concatenate([w_in[:, :EV_KR], kr_blk], axis=1)

    wq = w_uq.reshape(MLA_Q_RANK, MLA_HEADS, MLA_NOPE + MLA_ROPE)
    nope, r1, r2 = wq[..., :MLA_NOPE], wq[..., MLA_NOPE:MLA_NOPE + half], wq[..., MLA_NOPE + half:]
    zq = z(MLA_Q_RANK, MLA_HEADS, tail)
    w_q = jnp.concatenate([nope, r1, r2, r1, zq], axis=-1).reshape(MLA_Q_RANK, MLA_PAD)

    wkv = w_ukv.reshape(MLA_KV_RANK, MLA_HEADS, MLA_NOPE + MLA_V)
    k_nope, v = wkv[..., :MLA_NOPE], wkv[..., MLA_NOPE:]
    w_k = jnp.concatenate([k_nope, z(MLA_KV_RANK, MLA_HEADS, HEAD_PAD - MLA_NOPE)], axis=-1)
    zv = jnp.zeros_like(v)
    odd_head = (jnp.arange(MLA_HEADS) % 2 == 1)[None, :, None]
    w_v = jnp.concatenate([jnp.where(odd_head, zv, v), jnp.where(odd_head, v, zv)], axis=-1)
    return {
        "w_in": w_in_pad.astype(BF16), "conv_w": conv_w, "q_norm_g": q_norm_g[None, :],
        "kv_norm_g": kv_norm_g[None, :], "w_q": w_q.astype(BF16),
        "w_k": w_k.reshape(MLA_KV_RANK, MLA_PAD).astype(BF16),
        "w_v": w_v.reshape(MLA_KV_RANK, MLA_PAD).T.astype(BF16),
    }


def _block_diag(w):
    n, r, c = w.shape
    eye = jnp.eye(n, dtype=w.dtype)
    return (eye[:, None, :, None] * w[:, :, None, :]).reshape(n * r, n * c)


def _odd_weights(w_in, gate_up, gate_bias, gla_norm_g, conv_w, conv_b, w_a, b_a, w_i, b_i, lam):
    dk, dv = GLA_HEADS * GLA_DK, GLA_HEADS * GLA_DV
    o = 0
    parts = {}
    for name, width in (("q", dk), ("k", dk), ("v", dv), ("g", GLA_GATE_RANK), ("og", dv), ("x", LRU_WIDTH),
                        ("y", LRU_WIDTH)):
        parts[name] = w_in[:, o:o + width]
        o += width
    g_pad = jnp.zeros((D_MODEL, LANES - GLA_GATE_RANK), F32)
    w_in_pad = jnp.concatenate([parts["x"], parts["y"], parts["v"], parts["og"], parts["q"], parts["k"], parts["g"],
                                g_pad], axis=1)
    gate_up_pad = jnp.concatenate([gate_up, jnp.zeros((LANES - GLA_GATE_RANK, dk), F32)], axis=0)
    return {
        "w_in": w_in_pad.astype(BF16), "gate_up": gate_up_pad.astype(BF16), "gate_bias": gate_bias[None, :],
        "gla_norm_g": gla_norm_g[None, :], "conv_w": conv_w, "conv_b": conv_b[None, :],
        "w_a": _block_diag(w_a).astype(BF16), "b_a": b_a[None, :], "w_i": _block_diag(w_i).astype(BF16),
        "b_i": b_i[None, :], "lam": lam[None, :],
    }


def kernel(x, positions, mixer_norm_g, mlp_norm_g, mlp_w1, mlp_w2, final_norm_g, ev_w_in, ev_conv_w, mla_q_norm_g, mla_w_uq, mla_kv_norm_g, mla_w_ukv, ev_w_out, od_w_in, gla_w_gate_up, gla_b_gate, gla_norm_g, lru_conv_w, lru_conv_b, lru_w_a, lru_b_a, lru_w_i, lru_b_i, lru_lambda, od_w_out):
    batch, seq, d_model = x.shape
    depth = mixer_norm_g.shape[0]
    assert d_model == D_MODEL and seq % ROW_BLOCK == 0 and ROW_BLOCK % GLA_BLOCK == 0
    assert ATTN_BLOCK == ROW_BLOCK
    h = x.reshape(batch * seq, D_MODEL)
    cos, sin = _rope_tables(positions)
    final_g = final_norm_g[None, :]
    w1_bf, w2_bf = mlp_w1.astype(BF16), mlp_w2.astype(BF16)
    ev_w_out_bf, od_w_out_bf = ev_w_out.astype(BF16), od_w_out.astype(BF16)
    for layer in range(depth):
        j = layer // 2
        norm_g = mixer_norm_g[layer][None, :]
        if layer % 2 == 0:
            w = _even_weights(ev_w_in[j], ev_conv_w[j], mla_q_norm_g[j], mla_w_uq[j], mla_kv_norm_g[j], mla_w_ukv[j])
            mix_a, q, k, v = _even_pre(h, norm_g, w, cos, sin, batch, seq)
            mix_b = _attention(q, k, v, batch, seq)
            w_out = ev_w_out_bf
        else:
            w = _odd_weights(od_w_in[j], gla_w_gate_up[j], gla_b_gate[j], gla_norm_g[j], lru_conv_w[j],
                             lru_conv_b[j], lru_w_a[j], lru_b_a[j], lru_w_i[j], lru_b_i[j], lru_lambda[j])
            mix_a, mix_b = _odd_pre(h, norm_g, w, batch, seq)
            w_out = od_w_out_bf
        h = _post(h, mix_a, mix_b, w_out, j, mlp_norm_g[layer][None, :], w1_bf, w2_bf, layer, final_g,
                  final_norm=(layer == depth - 1))
    return h.reshape(batch, seq, D_MODEL)
```
